```python
import jax
import jax.numpy as jnp
from jax import lax
import numpy as np

D_MODEL = 1024
BATCH = 4
SEQ = 4096
DEPTH = 1
DEC_BATCH = 128
DEC_SEQ = 8
PAST_LEN = 16384
PAGE_SIZE = 128

N_HEADS = 8
QK_NOPE = 64
ROPE_DIM = 32
QK_HEAD = QK_NOPE + ROPE_DIM
V_HEAD = 64
Q_LORA = 384
KV_LORA = 256
ATTN_WIDTH = N_HEADS * V_HEAD
ROPE_THETA = 10000.0
SM_SCALE = QK_HEAD ** -0.5
Q_BLOCK = 128
POOL_WINDOWS = (2, 4, 8, 16)
POOL_GROUPS = len(POOL_WINDOWS)
POOL_GROUP_DIM = D_MODEL // 8
POOL_WIDTH = POOL_GROUPS * POOL_GROUP_DIM
POOL_PREV = max(POOL_WINDOWS) - 1
MIX_WIDTH = ATTN_WIDTH + POOL_WIDTH
IN_SPLITS = (Q_LORA, Q_LORA + KV_LORA, Q_LORA + KV_LORA + ROPE_DIM)
IN_COLS = Q_LORA + KV_LORA + ROPE_DIM + POOL_WIDTH
N_EXPERTS = 32
TOP_K = 4
D_FF = D_MODEL
SWIGLU_LIMIT = 7.0
SWIGLU_ALPHA = 1.702
EXPERT_BLOCK = 128
PLE_DIM = 256
EPS = 1e-6

kernel_name = 'mla_pool_hybrid_moe_decode_step'


def rmsnorm(x, g):
    xf = x.astype(jnp.float32)
    y = xf * lax.rsqrt(jnp.mean(xf * xf, axis=-1, keepdims=True) + EPS)
    return (y * g.astype(jnp.float32)).astype(x.dtype)


def rope(x, pos):
    half = ROPE_DIM // 2
    inv = ROPE_THETA ** (-jnp.arange(half, dtype=jnp.float32) / half)
    ang = pos.astype(jnp.float32)[:, None] * inv
    cos, sin = jnp.cos(ang)[:, None, :], jnp.sin(ang)[:, None, :]
    xf = x.astype(jnp.float32)
    x1, x2 = xf[..., :half], xf[..., half:]
    return jnp.concatenate([x1 * cos - x2 * sin, x2 * cos + x1 * sin], axis=-1).astype(x.dtype)


def mla_queries(cq, pos, w_q_b, g_q_head):
    q = jnp.einsum('...lc,chd->...lhd', cq, w_q_b)
    q = rmsnorm(q, g_q_head)
    return jnp.concatenate([q[..., :QK_NOPE], rope(q[..., QK_NOPE:], pos)], axis=-1)


def mla_keys(ckv, kpe, pos, w_kv_b, g_k_head):
    kv = jnp.einsum('...lc,chd->...lhd', ckv, w_kv_b)
    k_pe = jnp.broadcast_to(kpe[..., None, :], kv.shape[:-1] + (ROPE_DIM,))
    k = rmsnorm(jnp.concatenate([kv[..., :QK_NOPE], k_pe], axis=-1), g_k_head)
    k = jnp.concatenate([k[..., :QK_NOPE], rope(k[..., QK_NOPE:], pos)], axis=-1)
    return k, kv[..., QK_NOPE:]


def attend(q, k, v, q_pos, k_pos):
    s = jnp.einsum('...qhd,...khd->...hqk', q, k, preferred_element_type=jnp.float32) * SM_SCALE
    mask = k_pos[None, :] <= q_pos[:, None]
    s = jnp.where(mask, s, jnp.finfo(jnp.float32).min)
    p = jax.nn.softmax(s, axis=-1)
    return jnp.einsum('...hqk,...khd->...qhd', p.astype(v.dtype), v)


def mixer_inputs(x, lw):
    h = rmsnorm(x, lw['g_mix'])
    z = h @ lw['w_in']
    cq, ckv, kpe, u = jnp.split(z, IN_SPLITS, axis=-1)
    return rmsnorm(cq, lw['g_q_a']), rmsnorm(ckv, lw['g_kv_a']), kpe, u


def prompt_attention(cq, ckv, kpe, lw):
    b, s = cq.shape[0], cq.shape[1]
    pos = jnp.arange(s, dtype=jnp.int32)
    q = mla_queries(cq, pos, lw['w_q_b'], lw['g_q_head'])
    k, v = mla_keys(ckv, kpe, pos, lw['w_kv_b'], lw['g_k_head'])
    n_blk = s // Q_BLOCK
    qb = q.reshape(b, n_blk, Q_BLOCK, N_HEADS, QK_HEAD).swapaxes(0, 1)
    pb = pos.reshape(n_blk, Q_BLOCK)
    out = lax.map(lambda a: attend(a[0], k, v, a[1], pos), (qb, pb))
    return out.swapaxes(0, 1).reshape(b, s, N_HEADS, V_HEAD)


def sample_attention(cq, ckv, kpe, page_table, cache_ckv_l, cache_kpe_l, lw):
    n_new = cq.shape[1]
    q_pos = PAST_LEN + jnp.arange(n_new, dtype=jnp.int32)
    k_pos = jnp.arange(PAST_LEN + n_new, dtype=jnp.int32)
    q = mla_queries(cq, q_pos, lw['w_q_b'], lw['g_q_head'])

    def one_seq(args):
        pt, q_s, ckv_s, kpe_s = args
        ckv_all = jnp.concatenate([cache_ckv_l[pt].reshape(-1, KV_LORA), ckv_s], axis=0)
        kpe_all = jnp.concatenate([cache_kpe_l[pt].reshape(-1, ROPE_DIM), kpe_s], axis=0)
        k, v = mla_keys(ckv_all, kpe_all, k_pos, lw['w_kv_b'], lw['g_k_head'])
        return attend(q_s, k, v, q_pos, k_pos)

    return lax.map(one_seq, (page_table, q, ckv, kpe))


def multiscale_pool(u_prev, u, pos0, w_pool, pool_scale):
    n, l, c = u.shape
    buf = jnp.concatenate([u_prev.astype(u.dtype), u], axis=1)
    cs = jnp.cumsum(buf.astype(jnp.float32), axis=1)
    cs = jnp.concatenate([jnp.zeros((n, 1, c), jnp.float32), cs], axis=1)
    pos = pos0 + jnp.arange(l, dtype=jnp.int32)
    diffs = []
    for g, w in enumerate(POOL_WINDOWS):
        sl = slice(g * POOL_GROUP_DIM, (g + 1) * POOL_GROUP_DIM)
        win_sum = cs[:, POOL_PREV + 1:POOL_PREV + 1 + l, sl] - cs[:, POOL_PREV + 1 - w:POOL_PREV + 1 - w + l, sl]
        count = jnp.minimum(w, pos + 1).astype(jnp.float32)[None, :, None]
        diffs.append(win_sum / count - u[:, :, sl].astype(jnp.float32))
    d = jnp.stack(diffs, axis=2).astype(u.dtype)
    out = jnp.einsum('nlgc,gcd->nlgd', d, w_pool).reshape(n, l, c) * pool_scale
    return out, buf[:, -POOL_PREV:, :]


def moe_ffn(h, w_router, b_router, w_gate, b_gate, w_up, b_up, w_down, b_down):
    lead = h.shape[:-1]
    xt = h.reshape(-1, D_MODEL)
    n = xt.shape[0]
    logits = xt.astype(jnp.float32) @ w_router.astype(jnp.float32) + b_router.astype(jnp.float32)
    top_val, top_idx = lax.top_k(logits, TOP_K)
    gates = jax.nn.softmax(top_val, axis=-1)
    n_assign = n * TOP_K
    n_rows = ((n_assign + EXPERT_BLOCK - 1) // EXPERT_BLOCK + N_EXPERTS) * EXPERT_BLOCK
    n_blocks = n_rows // EXPERT_BLOCK
    e_flat = top_idx.reshape(-1)
    t_flat = jnp.repeat(jnp.arange(n, dtype=jnp.int32), TOP_K)
    order = jnp.argsort(e_flat)
    e_sorted = e_flat[order]
    counts = jnp.zeros((N_EXPERTS,), jnp.int32).at[e_flat].add(1)
    padded = (counts + EXPERT_BLOCK - 1) // EXPERT_BLOCK * EXPERT_BLOCK
    ends = jnp.cumsum(padded)
    rank = jnp.arange(n_assign, dtype=jnp.int32) - (jnp.cumsum(counts) - counts)[e_sorted]
    dest = (ends - padded)[e_sorted] + rank
    row_token = jnp.full((n_rows,), n, jnp.int32).at[dest].set(t_flat[order])
    row_gate = jnp.zeros((n_rows,), jnp.float32).at[dest].set(gates.reshape(-1)[order])
    block_start = jnp.arange(n_blocks, dtype=jnp.int32) * EXPERT_BLOCK
    block_expert = jnp.minimum(jnp.searchsorted(ends, block_start, side='right'), N_EXPERTS - 1)
    x_rows = jnp.concatenate([xt, jnp.zeros((1, D_MODEL), xt.dtype)], axis=0)[row_token]

    def expert_block(args):
        xb, e = args
        g = jnp.minimum(xb @ w_gate[e] + b_gate[e], SWIGLU_LIMIT)
        u = jnp.clip(xb @ w_up[e] + b_up[e], -SWIGLU_LIMIT, SWIGLU_LIMIT)
        return ((u + 1.0) * (g * jax.nn.sigmoid(SWIGLU_ALPHA * g))) @ w_down[e] + b_down[e]

    y_rows = lax.map(expert_block, (x_rows.reshape(n_blocks, EXPERT_BLOCK, D_MODEL), block_expert))
    y_rows = y_rows.reshape(n_rows, D_MODEL).astype(jnp.float32) * row_gate[:, None]
    out = jax.ops.segment_sum(y_rows, row_token, num_segments=n + 1)[:n]
    return out.astype(h.dtype).reshape(*lead, D_MODEL)


def finish_layer(x, attn_out, pool_out, p, lw):
    attn_flat = attn_out.reshape(attn_out.shape[:-2] + (ATTN_WIDTH,))
    x = x + jnp.concatenate([attn_flat, pool_out], axis=-1) @ lw['w_out']
    x = x + moe_ffn(rmsnorm(x, lw['g_ffn']), lw['w_router'], lw['b_router'], lw['w_gate'], lw['b_gate'],
                    lw['w_up'], lw['b_up'], lw['w_down'], lw['b_down'])
    gate = jax.nn.sigmoid(rmsnorm(x, lw['g_ple']) @ lw['w_ple_gate'])
    return x + gate * (p @ lw['w_ple_proj'])


def setup_inputs(seed: int = 0) -> dict:
    key = jax.random.key(seed)
    ks = list(jax.random.split(key, 40))
    f32 = jnp.float32

    def normal(shape, scale=1.0):
        return jax.random.normal(ks.pop(), shape, f32) * scale

    def gain(n):
        return 1.0 + normal((DEPTH, n), 0.05)

    n_pages = PAST_LEN // PAGE_SIZE
    n_used = DEC_BATCH * n_pages
    n_phys = n_used + (n_used + 3) // 4
    page_table = jax.random.permutation(ks.pop(), n_phys)[:n_used].reshape(DEC_BATCH, n_pages).astype(jnp.int32)
    return {
        'x_prompt': normal((BATCH, SEQ, D_MODEL)),
        'x_sample': normal((DEC_BATCH, DEC_SEQ, D_MODEL)),
        'cache_ckv': normal((DEPTH, n_phys, PAGE_SIZE, KV_LORA)),
        'cache_kpe': normal((DEPTH, n_phys, PAGE_SIZE, ROPE_DIM)),
        'state_pool': normal((DEPTH, DEC_BATCH, POOL_PREV, POOL_WIDTH)),
        'page_table': page_table,
        'p_prompt': normal((DEPTH, BATCH, SEQ, PLE_DIM)),
        'p_sample': normal((DEPTH, DEC_BATCH, DEC_SEQ, PLE_DIM)),
        'g_mix': gain(D_MODEL),
        'w_in': normal((DEPTH, D_MODEL, IN_COLS), D_MODEL ** -0.5),
        'g_q_a': gain(Q_LORA),
        'w_q_b': normal((DEPTH, Q_LORA, N_HEADS, QK_HEAD), Q_LORA ** -0.5),
        'g_kv_a': gain(KV_LORA),
        'w_kv_b': normal((DEPTH, KV_LORA, N_HEADS, QK_NOPE + V_HEAD), KV_LORA ** -0.5),
        'g_q_head': gain(QK_HEAD),
        'g_k_head': gain(QK_HEAD),
        'w_pool': normal((DEPTH, POOL_GROUPS, POOL_GROUP_DIM, POOL_GROUP_DIM), POOL_GROUP_DIM ** -0.5),
        'pool_scale': 1.0 + normal((DEPTH, POOL_WIDTH), 0.1),
        'w_out': normal((DEPTH, MIX_WIDTH, D_MODEL), MIX_WIDTH ** -0.5),
        'g_ffn': gain(D_MODEL),
        'w_router': normal((DEPTH, D_MODEL, N_EXPERTS), D_MODEL ** -0.5),
        'b_router': normal((DEPTH, N_EXPERTS), 0.01),
        'w_gate': normal((DEPTH, N_EXPERTS, D_MODEL, D_FF), D_MODEL ** -0.5),
        'b_gate': normal((DEPTH, N_EXPERTS, D_FF), 0.01),
        'w_up': normal((DEPTH, N_EXPERTS, D_MODEL, D_FF), D_MODEL ** -0.5),
        'b_up': normal((DEPTH, N_EXPERTS, D_FF), 0.01),
        'w_down': normal((DEPTH, N_EXPERTS, D_FF, D_MODEL), D_FF ** -0.5),
        'b_down': normal((DEPTH, N_EXPERTS, D_MODEL), 0.01),
        'g_ple': gain(D_MODEL),
        'w_ple_gate': normal((DEPTH, D_MODEL, D_MODEL), D_MODEL ** -0.5),
        'w_ple_proj': normal((DEPTH, PLE_DIM, D_MODEL), PLE_DIM ** -0.5),
    }


def reference(x_prompt, x_sample, cache_ckv, cache_kpe, state_pool, page_table, p_prompt, p_sample,
              g_mix, w_in, g_q_a, w_q_b, g_kv_a, w_kv_b, g_q_head, g_k_head, w_pool, pool_scale,
              w_out, g_ffn, w_router, b_router, w_gate, b_gate, w_up, b_up, w_down, b_down,
              g_ple, w_ple_gate, w_ple_proj):
    y_p, y_s = x_prompt, x_sample
    ckv_p, kpe_p, pool_p, ckv_s, kpe_s, pool_s = [], [], [], [], [], []
    for l in range(DEPTH):
        lw = dict(g_mix=g_mix[l], w_in=w_in[l], g_q_a=g_q_a[l], w_q_b=w_q_b[l], g_kv_a=g_kv_a[l],
                  w_kv_b=w_kv_b[l], g_q_head=g_q_head[l], g_k_head=g_k_head[l], w_out=w_out[l],
                  g_ffn=g_ffn[l], w_router=w_router[l], b_router=b_router[l], w_gate=w_gate[l],
                  b_gate=b_gate[l], w_up=w_up[l], b_up=b_up[l], w_down=w_down[l], b_down=b_down[l],
                  g_ple=g_ple[l], w_ple_gate=w_ple_gate[l], w_ple_proj=w_ple_proj[l])
        cq, ckv, kpe, u = mixer_inputs(y_p, lw)
        attn = prompt_attention(cq, ckv, kpe, lw)
        u_prev = jnp.zeros((u.shape[0], POOL_PREV, POOL_WIDTH), u.dtype)
        pool, pst = multiscale_pool(u_prev, u, 0, w_pool[l], pool_scale[l])
        y_p = finish_layer(y_p, attn, pool, p_prompt[l], lw)
        ckv_p.append(ckv)
        kpe_p.append(kpe)
        pool_p.append(pst)
        cq, ckv, kpe, u = mixer_inputs(y_s, lw)
        attn = sample_attention(cq, ckv, kpe, page_table, cache_ckv[l], cache_kpe[l], lw)
        pool, pst = multiscale_pool(state_pool[l], u, PAST_LEN, w_pool[l], pool_scale[l])
        y_s = finish_layer(y_s, attn, pool, p_sample[l], lw)
        ckv_s.append(ckv)
        kpe_s.append(kpe)
        pool_s.append(pst)
    return (y_p, y_s, jnp.stack(ckv_p), jnp.stack(kpe_p), jnp.stack(pool_p),
            jnp.stack(ckv_s), jnp.stack(kpe_s), jnp.stack(pool_s))
```

```python
import functools

import jax
import jax.numpy as jnp
from jax import lax
from jax.experimental import pallas as pl
from jax.experimental.pallas import tpu as pltpu

F32 = jnp.float32
BF16 = jnp.bfloat16

D_MODEL = 1024
N_HEADS = 8
QK_NOPE = 64
ROPE_DIM = 32
ROPE_HALF = ROPE_DIM // 2
QK_HEAD = QK_NOPE + ROPE_DIM
V_HEAD = 64
Q_LORA = 384
KV_LORA = 256
ATTN_WIDTH = N_HEADS * V_HEAD
ROPE_THETA = 10000.0
SM_SCALE = QK_HEAD ** -0.5
POOL_WINDOWS = (2, 4, 8, 16)
POOL_GROUP_DIM = 128
POOL_WIDTH = 512
POOL_PREV = 15
N_EXPERTS = 32
TOP_K = 4
SWIGLU_LIMIT = 7.0
SWIGLU_ALPHA = 1.702
PLE_DIM = 256
EPS = 1e-6

LANES = 128
HALO = 16
MAIN_COLS = Q_LORA + KV_LORA + POOL_WIDTH + LANES
VMEM_LIMIT = 56 * 1024 * 1024

TM_PROJ = 256
TQ, TK = 256, 512
PAGES_PER_STEP = 8
SUB = 256
TM_TOK = 256
TM_DISP = 512
TB = 256

NEG_INF = float("-inf")


def _dot(a, b):
    return jnp.dot(a, b, preferred_element_type=F32)


def _dot_nt(a, b):
    return lax.dot_general(a, b, (((1,), (1,)), ((), ())), preferred_element_type=F32)


def _rms(x, n):
    return x * lax.rsqrt(jnp.sum(x * x, axis=-1, keepdims=True) * (1.0 / n) + EPS)


def _rope_rows(x, c_tab, sa_tab, sb_tab):
    return (x * c_tab + pltpu.roll(x, LANES - ROPE_HALF, 1) * sa_tab
            + pltpu.roll(x, ROPE_HALF, 1) * sb_tab)


def _proj_body(*refs, tm, seq_mode):
    if seq_mode:
        (x_ref, c_ref, sa_ref, sb_ref, gmix_ref, wmain_ref, gqa_ref, gkva_ref, wq_ref, gq_ref,
         wk_ref, wv_ref, gk_ref, wpool_ref, pscale_ref, woutp_ref,
         ckv_ref, kpe_ref, q_ref, k_ref, v_ref, xmid_ref, pst_ref, ubuf) = refs
    else:
        (x_ref, c_ref, sa_ref, sb_ref, gmix_ref, wmain_ref, gqa_ref, gkva_ref, wq_ref, gq_ref,
         wkt_ref, gkn_ref,
         ckv_ref, kpe_ref, u_ref, q_ref, qlat_ref) = refs

    x = x_ref[...]
    h = (_rms(x, D_MODEL) * gmix_ref[...]).astype(BF16)
    z = _dot(h, wmain_ref[...])
    cqn = (_rms(z[:, :Q_LORA], Q_LORA) * gqa_ref[...]).astype(BF16)
    ckvn = _rms(z[:, Q_LORA:Q_LORA + KV_LORA], KV_LORA) * gkva_ref[...]
    u = z[:, Q_LORA + KV_LORA:Q_LORA + KV_LORA + POOL_WIDTH]
    kpe128 = z[:, Q_LORA + KV_LORA + POOL_WIDTH:]
    ckv_ref[...] = ckvn
    kpe_ref[...] = pltpu.roll(kpe128, LANES - QK_NOPE, 1)[:, :ROPE_DIM]

    c_tab, sa_tab, sb_tab = c_ref[...], sa_ref[...], sb_ref[...]
    q_all = _dot(cqn, wq_ref[...])
    gq = gq_ref[...]
    for hd in range(N_HEADS):
        qh = q_all[:, hd * LANES:(hd + 1) * LANES]
        qh = _rope_rows(_rms(qh, QK_HEAD) * gq, c_tab, sa_tab, sb_tab) * SM_SCALE
        if seq_mode:
            q_ref[hd] = qh.astype(BF16)
        else:
            q_ref[hd] = qh
            qlat_ref[hd] = _dot((qh * gkn_ref[...]).astype(BF16), wkt_ref[hd])

    if not seq_mode:
        u_ref[...] = u
        return

    ckvb = ckvn.astype(BF16)
    k_all = _dot(ckvb, wk_ref[...])
    gk = gk_ref[...]
    for hd in range(N_HEADS):
        kh = k_all[:, hd * LANES:(hd + 1) * LANES] + kpe128
        kh = _rope_rows(_rms(kh, QK_HEAD) * gk, c_tab, sa_tab, sb_tab)
        k_ref[hd] = kh.astype(BF16)
    v_all = _dot(ckvb, wv_ref[...])
    for hp in range(N_HEADS // 2):
        v_ref[hp] = v_all[:, hp * LANES:(hp + 1) * LANES].astype(BF16)

    j = pl.program_id(1)

    @pl.when(j == 0)
    def _():
        ubuf[0:HALO, :] = jnp.zeros((HALO, POOL_WIDTH), F32)

    ubuf[HALO:HALO + tm, :] = u
    pos1 = (j * tm + 1 + lax.broadcasted_iota(jnp.int32, (tm, 1), 0)).astype(F32)
    pool_parts = []
    for g, w in enumerate(POOL_WINDOWS):
        sl = slice(g * POOL_GROUP_DIM, (g + 1) * POOL_GROUP_DIM)
        ug = u[:, sl]
        acc = ug
        for back in range(1, w):
            acc = acc + ubuf[HALO - back:HALO - back + tm, sl]
        d = acc / jnp.minimum(float(w), pos1) - ug
        pool_parts.append(_dot(d.astype(BF16), wpool_ref[g]) * pscale_ref[:, sl])
    pool_out = jnp.concatenate(pool_parts, axis=1).astype(BF16)
    xmid_ref[...] = x + _dot(pool_out, woutp_ref[...])

    tail = ubuf[tm:tm + HALO, :]
    ubuf[0:HALO, :] = tail

    @pl.when(j == pl.num_programs(1) - 1)
    def _():
        pst_ref[...] = tail


def _full_spec(shape):
    nd = len(shape)
    return pl.BlockSpec(shape, lambda *_: (0,) * nd)


def _proj_prompt(x, tabs, wts, tm=TM_PROJ):
    b, s, _ = x.shape
    n = b * s
    nj = s // tm
    c_tab, sa_tab, sb_tab = tabs
    row = lambda width: pl.BlockSpec((tm, width), lambda bi, j: (bi * nj + j, 0))
    tab = pl.BlockSpec((tm, LANES), lambda bi, j: (j, 0))
    heads = lambda nh, width: pl.BlockSpec((nh, tm, width), lambda bi, j: (0, bi * nj + j, 0))
    in_specs = [pl.BlockSpec((None, tm, D_MODEL), lambda bi, j: (bi, j, 0)), tab, tab, tab] + [
        _full_spec(w.shape) for w in wts]
    out_shape = (
        jax.ShapeDtypeStruct((n, KV_LORA), F32),
        jax.ShapeDtypeStruct((n, ROPE_DIM), F32),
        jax.ShapeDtypeStruct((N_HEADS, n, LANES), BF16),
        jax.ShapeDtypeStruct((N_HEADS, n, LANES), BF16),
        jax.ShapeDtypeStruct((N_HEADS // 2, n, LANES), BF16),
        jax.ShapeDtypeStruct((n, D_MODEL), F32),
        jax.ShapeDtypeStruct((b, HALO, POOL_WIDTH), F32),
    )
    out_specs = (row(KV_LORA), row(ROPE_DIM), heads(N_HEADS, LANES), heads(N_HEADS, LANES),
                 heads(N_HEADS // 2, LANES), row(D_MODEL),
                 pl.BlockSpec((None, HALO, POOL_WIDTH), lambda bi, j: (bi, 0, 0)))
    return pl.pallas_call(
        functools.partial(_proj_body, tm=tm, seq_mode=True),
        out_shape=out_shape, grid=(b, nj), in_specs=in_specs, out_specs=out_specs,
        scratch_shapes=[pltpu.VMEM((tm + HALO, POOL_WIDTH), F32)],
        compiler_params=pltpu.CompilerParams(dimension_semantics=("arbitrary", "arbitrary"),
                                             vmem_limit_bytes=VMEM_LIMIT),
        name="proj_prompt",
    )(x, c_tab, sa_tab, sb_tab, *wts)


def _proj_sample(x, tabs, wts, tm=TM_PROJ):
    n = x.shape[0]
    c_tab, sa_tab, sb_tab = tabs
    row = lambda width: pl.BlockSpec((tm, width), lambda i: (i, 0))
    heads = lambda width: pl.BlockSpec((N_HEADS, tm, width), lambda i: (0, i, 0))
    in_specs = [row(D_MODEL), _full_spec((tm, LANES)), _full_spec((tm, LANES)), _full_spec((tm, LANES))] + [
        _full_spec(w.shape) for w in wts]
    out_shape = (
        jax.ShapeDtypeStruct((n, KV_LORA), F32),
        jax.ShapeDtypeStruct((n, ROPE_DIM), F32),
        jax.ShapeDtypeStruct((n, POOL_WIDTH), F32),
        jax.ShapeDtypeStruct((N_HEADS, n, LANES), F32),
        jax.ShapeDtypeStruct((N_HEADS, n, KV_LORA), F32),
    )
    out_specs = (row(KV_LORA), row(ROPE_DIM), row(POOL_WIDTH), heads(LANES), heads(KV_LORA))
    return pl.pallas_call(
        functools.partial(_proj_body, tm=tm, seq_mode=False),
        out_shape=out_shape, grid=(n // tm,), in_specs=in_specs, out_specs=out_specs,
        compiler_params=pltpu.CompilerParams(dimension_semantics=("arbitrary",),
                                             vmem_limit_bytes=VMEM_LIMIT),
        name="proj_sample",
    )(x, c_tab, sa_tab, sb_tab, *wts)


def _flash_body(q_ref, k_ref, v_ref, o_ref, m_sc, l_sc, acc_sc, *, tq, tk):
    qi = pl.program_id(2)
    n_full = (qi * tq) // tk
    row = qi * tq + lax.broadcasted_iota(jnp.int32, (tq, tk), 0)
    col = lax.broadcasted_iota(jnp.int32, (tq, tk), 1)
    outs = []
    for hh in range(2):
        q = q_ref[hh]
        m_sc[...] = jnp.full((tq, LANES), NEG_INF, F32)
        l_sc[...] = jnp.zeros((tq, LANES), F32)
        acc_sc[...] = jnp.zeros((tq, LANES), F32)

        def step(ki, masked):
            start = pl.multiple_of(ki * tk, tk)
            s = _dot_nt(q, k_ref[hh, pl.ds(start, tk), :])
            if masked:
                s = jnp.where(col + start <= row, s, NEG_INF)
            m_prev = m_sc[:, :1]
            m_new = jnp.maximum(m_prev, jnp.max(s, axis=-1, keepdims=True))
            alpha = jnp.exp(m_prev - m_new)
            p = jnp.exp(s - m_new)
            l_sc[...] = jnp.broadcast_to(alpha * l_sc[:, :1] + jnp.sum(p, axis=-1, keepdims=True),
                                         (tq, LANES))
            acc_sc[...] = alpha * acc_sc[...] + _dot(p.astype(BF16), v_ref[pl.ds(start, tk), :])
            m_sc[...] = jnp.broadcast_to(m_new, (tq, LANES))

        def loop_body(ki, carry):
            step(ki, False)
            return carry

        lax.fori_loop(0, n_full, loop_body, 0)
        step(n_full, True)
        outs.append(acc_sc[...] / l_sc[:, :1])
    lane = lax.broadcasted_iota(jnp.int32, (tq, LANES), 1)
    o_ref[...] = jnp.where(lane < V_HEAD, outs[0], outs[1]).astype(o_ref.dtype)


def _flash_prompt(q, k, v, b, s, tq=TQ, tk=TK):
    n = b * s
    nq = s // tq
    return pl.pallas_call(
        functools.partial(_flash_body, tq=tq, tk=tk),
        out_shape=jax.ShapeDtypeStruct((n, ATTN_WIDTH), BF16),
        grid=(b, N_HEADS // 2, nq),
        in_specs=[
            pl.BlockSpec((2, tq, LANES), lambda bi, hp, qi: (hp, bi * nq + qi, 0)),
            pl.BlockSpec((2, s, LANES), lambda bi, hp, qi: (hp, bi, 0)),
            pl.BlockSpec((None, s, LANES), lambda bi, hp, qi: (hp, bi, 0)),
        ],
        out_specs=pl.BlockSpec((tq, LANES), lambda bi, hp, qi: (bi * nq + qi, hp)),
        scratch_shapes=[pltpu.VMEM((tq, LANES), F32)] * 3,
        compiler_params=pltpu.CompilerParams(dimension_semantics=("arbitrary",) * 3,
                                             vmem_limit_bytes=VMEM_LIMIT),
        name="flash_prompt",
    )(q, k, v)


def _pool_sample_body(st_ref, u_ref, wpool_ref, pscale_ref, o_ref, *, n_new):
    for t in range(n_new):
        for g, w in enumerate(POOL_WINDOWS):
            sl = slice(g * POOL_GROUP_DIM, (g + 1) * POOL_GROUP_DIM)
            ug = u_ref[t, :, sl]
            acc = ug
            for back in range(1, w):
                src = t - back
                acc = acc + (u_ref[src, :, sl] if src >= 0 else st_ref[POOL_PREV + src, :, sl])
            d = acc / float(w) - ug
            o_ref[t, :, sl] = _dot(d.astype(BF16), wpool_ref[g]) * pscale_ref[:, sl]


def _pool_sample(state_t, u_t, wpool, pscale):
    n_new, nb, _ = u_t.shape
    return pl.pallas_call(
        functools.partial(_pool_sample_body, n_new=n_new),
        out_shape=jax.ShapeDtypeStruct((n_new, nb, POOL_WIDTH), F32),
        grid=(1,),
        in_specs=[_full_spec(state_t.shape), _full_spec(u_t.shape), _full_spec(wpool.shape),
                  _full_spec(pscale.shape)],
        out_specs=_full_spec((n_new, nb, POOL_WIDTH)),
        compiler_params=pltpu.CompilerParams(dimension_semantics=("arbitrary",),
                                             vmem_limit_bytes=VMEM_LIMIT),
        name="pool_sample",
    )(state_t, u_t, wpool, pscale)


def _sample_attn_body(pt_ref, *refs, n_pages, n_new):
    npg = PAGES_PER_STEP
    ckv_refs = refs[:npg]
    kpe_refs = refs[npg:2 * npg]
    (qlat_ref, q_ref, ckvn_ref, kpen_ref, cos_ref, sin_ref, cost_ref, sint_ref,
     wkt_ref, sel_ref, gpe_ref, wv_ref, o_ref, a_sc, m_sc, l_sc, acc_sc) = refs[2 * npg:]
    del pt_ref
    jc = pl.program_id(1)
    rows = N_HEADS * n_new

    @pl.when(jc == 0)
    def _():
        a_sc[0:N_HEADS * QK_NOPE, :] = wkt_ref[...]
        a_sc[N_HEADS * QK_NOPE:, :] = qlat_ref[...].reshape(rows, KV_LORA).astype(BF16)
        m_sc[...] = jnp.full((rows, LANES), NEG_INF, F32)
        l_sc[...] = jnp.zeros((rows, LANES), F32)
        acc_sc[...] = jnp.zeros((rows, KV_LORA), F32)

    qpe = q_ref[...].reshape(rows, LANES).astype(BF16)

    def scores(cb, kp, cos, sin):
        ck = cb.shape[0]
        r = _dot_nt(a_sc[...], cb)
        kt = r[:N_HEADS * QK_NOPE]
        ssn = jnp.sum((kt * kt).reshape(N_HEADS, QK_NOPE, ck), axis=1)
        kpt = _dot_nt(sel_ref[...], kp)
        x1 = kpt[QK_NOPE:QK_NOPE + ROPE_HALF]
        x2 = kpt[QK_NOPE + ROPE_HALF:QK_NOPE + ROPE_DIM]
        ssp = jnp.sum(x1 * x1 + x2 * x2, axis=0, keepdims=True)
        rs = lax.rsqrt((ssn + ssp) * (1.0 / QK_HEAD) + EPS)
        reps = ck // LANES
        g1 = jnp.tile(gpe_ref[0:ROPE_HALF, :], (1, reps))
        g2 = jnp.tile(gpe_ref[ROPE_HALF:ROPE_DIM, :], (1, reps))
        x1 = x1 * g1
        x2 = x2 * g2
        rot = jnp.concatenate([
            jnp.zeros((QK_NOPE, ck), F32), x1 * cos - x2 * sin, x2 * cos + x1 * sin,
            jnp.zeros((LANES - QK_HEAD, ck), F32)], axis=0).astype(BF16)
        s = r[N_HEADS * QK_NOPE:] + _dot(qpe, rot)
        s = s.reshape(N_HEADS, n_new, ck) * rs[:, None, :]
        return s.reshape(rows, ck)

    def update(s_all, cbs):
        m_prev = m_sc[:, :1]
        m_new = jnp.maximum(m_prev, jnp.max(s_all, axis=-1, keepdims=True))
        alpha = jnp.exp(m_prev - m_new)
        p = jnp.exp(s_all - m_new)
        l_sc[...] = jnp.broadcast_to(alpha * l_sc[:, :1] + jnp.sum(p, axis=-1, keepdims=True),
                                     (rows, LANES))
        acc = alpha * acc_sc[...]
        off = 0
        for cb in cbs:
            ck = cb.shape[0]
            acc = acc + _dot(p[:, off:off + ck].astype(BF16), cb)
            off += ck
        acc_sc[...] = acc
        m_sc[...] = jnp.broadcast_to(m_new, (rows, LANES))

    pages_per_sub = SUB // LANES
    cbs, s_parts = [], []
    for sc in range(npg // pages_per_sub):
        pg = range(sc * pages_per_sub, (sc + 1) * pages_per_sub)
        cb = jnp.concatenate([ckv_refs[i][...].astype(BF16) for i in pg], axis=0)
        kp = jnp.concatenate([kpe_refs[i][...].astype(BF16) for i in pg], axis=0)
        cos = cos_ref[:, sc * SUB:(sc + 1) * SUB]
        sin = sin_ref[:, sc * SUB:(sc + 1) * SUB]
        cbs.append(cb)
        s_parts.append(scores(cb, kp, cos, sin))
    update(jnp.concatenate(s_parts, axis=1), cbs)

    @pl.when(jc == n_pages // npg - 1)
    def _():
        pad = LANES - n_new
        cb = jnp.concatenate([ckvn_ref[...], jnp.zeros((pad, KV_LORA), F32)], axis=0).astype(BF16)
        kp = jnp.concatenate([kpen_ref[...], jnp.zeros((pad, ROPE_DIM), F32)], axis=0).astype(BF16)
        s = scores(cb, kp, cost_ref[...], sint_ref[...])
        key = lax.broadcasted_iota(jnp.int32, (rows, LANES), 1)
        tok = lax.broadcasted_iota(jnp.int32, (rows, LANES), 0) % n_new
        update(jnp.where(key <= tok, s, NEG_INF), [cb])
        o_lat = (acc_sc[...] / l_sc[:, :1]).astype(BF16)
        full = _dot(o_lat, wv_ref[...]).reshape(N_HEADS, n_new, ATTN_WIDTH)
        hd = lax.broadcasted_iota(jnp.int32, (N_HEADS, n_new, ATTN_WIDTH), 0)
        colh = lax.broadcasted_iota(jnp.int32, (N_HEADS, n_new, ATTN_WIDTH), 2) // V_HEAD
        o_ref[...] = jnp.sum(jnp.where(hd == colh, full, 0.0), axis=0)


def _sample_attn(page_table, cache_ckv, cache_kpe, qlat, qf, ckv_new, kpe_new, tabs, wts):
    nb, n_pages = page_table.shape
    page = cache_ckv.shape[1]
    n_new = ckv_new.shape[1]
    npg = PAGES_PER_STEP
    cos_t, sin_t, cos_tail, sin_tail = tabs
    rows = N_HEADS * n_new

    def page_spec(width, i):
        return pl.BlockSpec((None, page, width), lambda b, jc, pt: (pt[b, jc * npg + i], 0, 0))

    in_specs = ([page_spec(KV_LORA, i) for i in range(npg)] + [page_spec(ROPE_DIM, i) for i in range(npg)] + [
        pl.BlockSpec((N_HEADS, n_new, KV_LORA), lambda b, jc, pt: (0, b, 0)),
        pl.BlockSpec((N_HEADS, n_new, LANES), lambda b, jc, pt: (0, b, 0)),
        pl.BlockSpec((None, n_new, KV_LORA), lambda b, jc, pt: (b, 0, 0)),
        pl.BlockSpec((None, n_new, ROPE_DIM), lambda b, jc, pt: (b, 0, 0)),
        pl.BlockSpec((ROPE_HALF, npg * page), lambda b, jc, pt: (0, jc)),
        pl.BlockSpec((ROPE_HALF, npg * page), lambda b, jc, pt: (0, jc)),
        pl.BlockSpec((ROPE_HALF, LANES), lambda b, jc, pt: (0, 0)),
        pl.BlockSpec((ROPE_HALF, LANES), lambda b, jc, pt: (0, 0)),
    ] + [pl.BlockSpec(w.shape, lambda b, jc, pt, nd=len(w.shape): (0,) * nd) for w in wts])
    grid_spec = pltpu.PrefetchScalarGridSpec(
        num_scalar_prefetch=1, grid=(nb, n_pages // npg), in_specs=in_specs,
        out_specs=pl.BlockSpec((None, n_new, ATTN_WIDTH), lambda b, jc, pt: (b, 0, 0)),
        scratch_shapes=[pltpu.VMEM((N_HEADS * QK_NOPE + rows, KV_LORA), BF16),
                        pltpu.VMEM((rows, LANES), F32), pltpu.VMEM((rows, LANES), F32),
                        pltpu.VMEM((rows, KV_LORA), F32)])
    return pl.pallas_call(
        functools.partial(_sample_attn_body, n_pages=n_pages, n_new=n_new),
        out_shape=jax.ShapeDtypeStruct((nb, n_new, ATTN_WIDTH), F32),
        grid_spec=grid_spec,
        compiler_params=pltpu.CompilerParams(dimension_semantics=("arbitrary", "arbitrary"),
                                             vmem_limit_bytes=VMEM_LIMIT),
        name="sample_attn",
    )(page_table, *([cache_ckv] * npg), *([cache_kpe] * npg), qlat, qf, ckv_new, kpe_new,
      cos_t, sin_t, cos_tail, sin_tail, *wts)


def _post_attn_body(xmid_ref, attn_p_ref, xs_ref, attn_s_ref, pool_s_ref, wouta_ref, woutp_ref, gffn_ref,
                    wrh_ref, wrl_ref, br_ref,
                    x1_ref, h2_ref, topi_ref, gate_ref, rank_ref, cnt_ref, base_sc, xm_sc, at_sc,
                    *, tm, n_prompt_tiles):
    i = pl.program_id(0)

    @pl.when(i == 0)
    def _():
        base_sc[...] = jnp.zeros((N_EXPERTS, LANES), F32)

    @pl.when(i < n_prompt_tiles)
    def _():
        xm_sc[...] = xmid_ref[...]
        at_sc[...] = attn_p_ref[...]

    @pl.when(i >= n_prompt_tiles)
    def _():
        xm_sc[...] = xs_ref[...] + _dot(pool_s_ref[...].astype(BF16), woutp_ref[...])
        at_sc[...] = attn_s_ref[...].astype(BF16)

    x1 = xm_sc[...] + _dot(at_sc[...], wouta_ref[...])
    x1_ref[...] = x1
    h2 = _rms(x1, D_MODEL) * gffn_ref[...]
    h2_ref[...] = h2
    h_hi = h2.astype(BF16)
    h_lo = (h2 - h_hi.astype(F32)).astype(BF16)
    lg = (_dot_nt(wrh_ref[...], h_hi) + _dot_nt(wrh_ref[...], h_lo) + _dot_nt(wrl_ref[...], h_hi)
          + br_ref[:, :1])
    eid = lax.broadcasted_iota(jnp.int32, (N_EXPERTS, tm), 0)
    vals, idxs, hots = [], [], []
    for _ in range(TOP_K):
        mx = jnp.max(lg, axis=0, keepdims=True)
        ix = jnp.min(jnp.where(lg == mx, eid, N_EXPERTS), axis=0, keepdims=True)
        hot = eid == ix
        lg = jnp.where(hot, NEG_INF, lg)
        vals.append(mx)
        idxs.append(ix)
        hots.append(hot)
    ex = [jnp.exp(v - vals[0]) for v in vals]
    den = ex[0] + ex[1] + ex[2] + ex[3]
    gate_ref[...] = jnp.concatenate([e / den for e in ex], axis=0)
    topi_ref[...] = jnp.concatenate(idxs, axis=0)

    anyhot = (hots[0] | hots[1] | hots[2] | hots[3])
    hot_f = jnp.where(anyhot, 1.0, 0.0)
    upper = jnp.where(lax.broadcasted_iota(jnp.int32, (tm, tm), 0) < lax.broadcasted_iota(jnp.int32, (tm, tm), 1),
                      1.0, 0.0).astype(BF16)
    before = _dot(hot_f.astype(BF16), upper) + base_sc[:, :1]
    ranks = [jnp.sum(jnp.where(h, before, 0.0), axis=0, keepdims=True) for h in hots]
    rank_ref[...] = jnp.concatenate(ranks, axis=0).astype(jnp.int32)
    base_sc[...] = base_sc[...] + jnp.sum(hot_f, axis=1, keepdims=True)
    cnt_ref[...] = base_sc[...].astype(jnp.int32)


def _post_attn(xmid_p, attn_p, x_s, attn_s, pool_s, wts, tm=TM_TOK):
    n_p, n_s = xmid_p.shape[0], x_s.shape[0]
    npt, nst = n_p // tm, n_s // tm
    nt = n_p + n_s
    p_idx = lambda i: (jnp.minimum(i, npt - 1), 0)
    s_idx = lambda i: (jnp.maximum(i - npt, 0), 0)
    in_specs = [
        pl.BlockSpec((tm, D_MODEL), p_idx), pl.BlockSpec((tm, ATTN_WIDTH), p_idx),
        pl.BlockSpec((tm, D_MODEL), s_idx), pl.BlockSpec((tm, ATTN_WIDTH), s_idx),
        pl.BlockSpec((tm, POOL_WIDTH), s_idx),
    ] + [_full_spec(w.shape) for w in wts]
    col = lambda rows: pl.BlockSpec((rows, tm), lambda i: (0, i))
    out_shape = (
        jax.ShapeDtypeStruct((nt, D_MODEL), F32), jax.ShapeDtypeStruct((nt, D_MODEL), F32),
        jax.ShapeDtypeStruct((TOP_K, nt), jnp.int32), jax.ShapeDtypeStruct((TOP_K, nt), F32),
        jax.ShapeDtypeStruct((TOP_K, nt), jnp.int32), jax.ShapeDtypeStruct((N_EXPERTS, LANES), jnp.int32),
    )
    out_specs = (pl.BlockSpec((tm, D_MODEL), lambda i: (i, 0)), pl.BlockSpec((tm, D_MODEL), lambda i: (i, 0)),
                 col(TOP_K), col(TOP_K), col(TOP_K), _full_spec((N_EXPERTS, LANES)))
    return pl.pallas_call(
        functools.partial(_post_attn_body, tm=tm, n_prompt_tiles=npt),
        out_shape=out_shape, grid=(npt + nst,), in_specs=in_specs, out_specs=out_specs,
        scratch_shapes=[pltpu.VMEM((N_EXPERTS, LANES), F32), pltpu.VMEM((tm, D_MODEL), F32),
                        pltpu.VMEM((tm, ATTN_WIDTH), BF16)],
        compiler_params=pltpu.CompilerParams(dimension_semantics=("arbitrary",),
                                             vmem_limit_bytes=VMEM_LIMIT),
        name="post_attn",
    )(xmid_p, attn_p, x_s, attn_s, pool_s, *wts)


def _row_copy(src_ref, src_row, dst_ref, dst_row, sem):
    return pltpu.make_async_copy(src_ref.at[pl.ds(src_row, 1)], dst_ref.at[pl.ds(dst_row, 1)], sem)


def _dispatch_body(dest_ref, h2_ref, xs_in_ref, xs_ref, sem, *, tm):
    del xs_in_ref
    base = pl.program_id(0) * tm

    def issue(t, carry):
        for k in range(TOP_K):
            _row_copy(h2_ref, base + t, xs_ref, dest_ref[k, t], sem).start()
        return carry

    lax.fori_loop(0, tm, issue, 0)

    def drain(t, carry):
        for k in range(TOP_K):
            _row_copy(h2_ref, 0, xs_ref, 0, sem).wait()
        return carry

    lax.fori_loop(0, tm, drain, 0)


def _dispatch(dest, h2, xs_init, tm=TM_DISP):
    nt = h2.shape[0]
    return pl.pallas_call(
        functools.partial(_dispatch_body, tm=tm),
        out_shape=jax.ShapeDtypeStruct(xs_init.shape, xs_init.dtype),
        grid=(nt // tm,),
        in_specs=[pl.BlockSpec((TOP_K, tm), lambda i: (0, i), memory_space=pltpu.SMEM),
                  pl.BlockSpec(memory_space=pl.ANY), pl.BlockSpec(memory_space=pl.ANY)],
        out_specs=pl.BlockSpec(memory_space=pl.ANY),
        scratch_shapes=[pltpu.SemaphoreType.DMA(())],
        input_output_aliases={2: 0},
        compiler_params=pltpu.CompilerParams(dimension_semantics=("arbitrary",), has_side_effects=True),
        name="dispatch",
    )(dest, h2, xs_init)


def _experts_body(be_ref, nu_ref, x_ref, wg_ref, bg_ref, wu_ref, bu_ref, wd_ref, bd_ref, y_ref):
    blk = pl.program_id(0)

    @pl.when(blk < nu_ref[0])
    def _():
        xb = x_ref[...].astype(BF16)
        g = jnp.minimum(_dot(xb, wg_ref[...]) + bg_ref[...], SWIGLU_LIMIT)
        u = jnp.clip(_dot(xb, wu_ref[...]) + bu_ref[...], -SWIGLU_LIMIT, SWIGLU_LIMIT)
        act = (u + 1.0) * (g * jax.nn.sigmoid(SWIGLU_ALPHA * g))
        y_ref[...] = _dot(act.astype(BF16), wd_ref[...]) + bd_ref[...]

    @pl.when(blk >= nu_ref[0])
    def _():
        y_ref[...] = jnp.zeros(y_ref.shape, F32)


def _experts(block_expert, n_used, xs, wg, bg, wu, bu, wd, bd, tb=TB):
    r = xs.shape[0]
    w_spec = pl.BlockSpec((None, D_MODEL, D_MODEL), lambda b, be, nu: (be[b], 0, 0))
    b_spec = pl.BlockSpec((None, 1, D_MODEL), lambda b, be, nu: (be[b], 0, 0))
    x_spec = pl.BlockSpec((tb, D_MODEL), lambda b, be, nu: (jnp.minimum(b, jnp.maximum(nu[0] - 1, 0)), 0))
    grid_spec = pltpu.PrefetchScalarGridSpec(
        num_scalar_prefetch=2, grid=(r // tb,),
        in_specs=[x_spec, w_spec, b_spec, w_spec, b_spec, w_spec, b_spec],
        out_specs=pl.BlockSpec((tb, D_MODEL), lambda b, be, nu: (b, 0)))
    return pl.pallas_call(
        _experts_body, out_shape=jax.ShapeDtypeStruct((r, D_MODEL), F32), grid_spec=grid_spec,
        compiler_params=pltpu.CompilerParams(dimension_semantics=("arbitrary",),
                                             vmem_limit_bytes=VMEM_LIMIT),
        name="experts",
    )(block_expert, n_used, xs, wg, bg, wu, bu, wd, bd)


def _final_body(dest_ref, x1_ref, gate_ref, p_ref, ys_ref, gple_ref, wpg_ref, wpp_ref, o_ref, gbuf, sem, *, tm):
    def issue(t, carry):
        for k in range(TOP_K):
            pltpu.make_async_copy(ys_ref.at[pl.ds(dest_ref[k, t], 1)], gbuf.at[k, pl.ds(t, 1)], sem).start()
        return carry

    lax.fori_loop(0, tm, issue, 0)
    pp = _dot(p_ref[...].astype(BF16), wpp_ref[...])

    def drain(t, carry):
        for k in range(TOP_K):
            pltpu.make_async_copy(ys_ref.at[pl.ds(0, 1)], gbuf.at[0, pl.ds(0, 1)], sem).wait()
        return carry

    lax.fori_loop(0, tm, drain, 0)
    gates = gate_ref[...]
    x2 = x1_ref[...]
    for k in range(TOP_K):
        x2 = x2 + gates[:, k:k + 1] * gbuf[k]
    hn = (_rms(x2, D_MODEL) * gple_ref[...]).astype(BF16)
    o_ref[...] = x2 + jax.nn.sigmoid(_dot(hn, wpg_ref[...])) * pp


def _final(dest, x1, gates_t, p, ys, wts, tile_off, n_tok, tm=TM_TOK):
    in_specs = [
        pl.BlockSpec((TOP_K, tm), lambda i: (0, i + tile_off), memory_space=pltpu.SMEM),
        pl.BlockSpec((tm, D_MODEL), lambda i: (i + tile_off, 0)),
        pl.BlockSpec((tm, TOP_K), lambda i: (i + tile_off, 0)),
        pl.BlockSpec((tm, PLE_DIM), lambda i: (i, 0)),
        pl.BlockSpec(memory_space=pl.ANY),
    ] + [_full_spec(w.shape) for w in wts]
    return pl.pallas_call(
        functools.partial(_final_body, tm=tm),
        out_shape=jax.ShapeDtypeStruct((n_tok, D_MODEL), F32),
        grid=(n_tok // tm,), in_specs=in_specs,
        out_specs=pl.BlockSpec((tm, D_MODEL), lambda i: (i, 0)),
        scratch_shapes=[pltpu.VMEM((TOP_K, tm, D_MODEL), F32), pltpu.SemaphoreType.DMA(())],
        compiler_params=pltpu.CompilerParams(dimension_semantics=("arbitrary",),
                                             vmem_limit_bytes=VMEM_LIMIT),
        name="final",
    )(dest, x1, gates_t, p, ys, *wts)


def _rope_angles(pos):
    inv = ROPE_THETA ** (-jnp.arange(ROPE_HALF, dtype=F32) / ROPE_HALF)
    ang = pos.astype(F32)[:, None] * inv
    return jnp.cos(ang), jnp.sin(ang)


def _rope_row_tables(pos):
    cos, sin = _rope_angles(pos)
    n = pos.shape[0]
    z = lambda w: jnp.zeros((n, w), F32)
    c_tab = jnp.concatenate([jnp.ones((n, QK_NOPE), F32), cos, cos, z(LANES - QK_HEAD)], axis=1)
    sa_tab = jnp.concatenate([z(QK_NOPE), -sin, z(ROPE_HALF), z(LANES - QK_HEAD)], axis=1)
    sb_tab = jnp.concatenate([z(QK_NOPE), z(ROPE_HALF), sin, z(LANES - QK_HEAD)], axis=1)
    return c_tab, sa_tab, sb_tab


def _pad_lanes(v, width=LANES):
    return jnp.pad(v, [(0, 0)] * (v.ndim - 1) + [(0, width - v.shape[-1])])


def kernel(x_prompt, x_sample, cache_ckv, cache_kpe, state_pool, page_table, p_prompt, p_sample, g_mix, w_in, g_q_a, w_q_b, g_kv_a, w_kv_b, g_q_head, g_k_head, w_pool, pool_scale, w_out, g_ffn, w_router, b_router, w_gate, b_gate, w_up, b_up, w_down, b_down, g_ple, w_ple_gate, w_ple_proj):
    assert x_prompt.shape[-1] == D_MODEL and g_mix.shape[0] == 1
    b, s, _ = x_prompt.shape
    nb, n_new, _ = x_sample.shape
    n_p, n_s = b * s, nb * n_new
    past = page_table.shape[1] * cache_ckv.shape[2]
    row = lambda v: v.reshape(1, -1).astype(F32)

    wi = w_in[0]
    w_cq, w_ckv = wi[:, :Q_LORA], wi[:, Q_LORA:Q_LORA + KV_LORA]
    w_kpe = wi[:, Q_LORA + KV_LORA:Q_LORA + KV_LORA + ROPE_DIM]
    w_u = wi[:, Q_LORA + KV_LORA + ROPE_DIM:]
    w_kpe128 = jnp.pad(w_kpe, ((0, 0), (QK_NOPE, LANES - QK_HEAD)))
    w_main = jnp.concatenate([w_cq, w_ckv, w_u, w_kpe128], axis=1).astype(BF16)
    wq = _pad_lanes(w_q_b[0]).reshape(Q_LORA, N_HEADS * LANES).astype(BF16)
    wk_nope = w_kv_b[0][:, :, :QK_NOPE]
    wk = _pad_lanes(wk_nope).reshape(KV_LORA, N_HEADS * LANES).astype(BF16)
    wv = w_kv_b[0][:, :, QK_NOPE:].reshape(KV_LORA, ATTN_WIDTH).astype(BF16)
    gq128 = _pad_lanes(row(g_q_head[0]))
    gk128 = _pad_lanes(row(g_k_head[0]))
    gkn128 = _pad_lanes(row(g_k_head[0][:QK_NOPE]))
    wkt = jnp.transpose(wk_nope, (1, 2, 0))
    wkt128 = jnp.pad(wkt, ((0, 0), (0, LANES - QK_NOPE), (0, 0))).astype(BF16)
    wkt_flat = wkt.reshape(N_HEADS * QK_NOPE, KV_LORA).astype(BF16)
    gpe_b = jnp.broadcast_to(g_k_head[0][QK_NOPE:].astype(F32)[:, None], (ROPE_DIM, LANES))
    sel = jnp.pad(jnp.eye(ROPE_DIM, dtype=F32), ((QK_NOPE, LANES - QK_HEAD), (0, 0))).astype(BF16)
    wpool = w_pool[0].astype(BF16)
    pscale = row(pool_scale[0])
    wout_a = w_out[0][:ATTN_WIDTH].astype(BF16)
    wout_p = w_out[0][ATTN_WIDTH:].astype(BF16)
    wr_t = w_router[0].T.astype(F32)
    wr_hi = wr_t.astype(BF16)
    wr_lo = (wr_t - wr_hi.astype(F32)).astype(BF16)
    br_b = jnp.broadcast_to(b_router[0].astype(F32)[:, None], (N_EXPERTS, LANES))

    tabs_p = _rope_row_tables(jnp.arange(s, dtype=jnp.int32))
    pos_s = past + (jnp.arange(TM_PROJ, dtype=jnp.int32) % n_new)
    tabs_s = _rope_row_tables(pos_s)
    cos_t, sin_t = (t.T for t in _rope_angles(jnp.arange(past + LANES, dtype=jnp.int32)))
    tabs_attn = (cos_t[:, :past], sin_t[:, :past], cos_t[:, past:], sin_t[:, past:])

    ckv_p, kpe_p, q_p, k_p, v_p, xmid_p, pst_p = _proj_prompt(
        x_prompt, tabs_p,
        (row(g_mix[0]), w_main, row(g_q_a[0]), row(g_kv_a[0]), wq, gq128, wk, wv, gk128, wpool, pscale, wout_p))
    attn_p = _flash_prompt(q_p, k_p, v_p, b, s)

    xs_flat = x_sample.reshape(n_s, D_MODEL)
    ckv_s, kpe_s, u_s, qf_s, qlat_s = _proj_sample(
        xs_flat, tabs_s, (row(g_mix[0]), w_main, row(g_q_a[0]), row(g_kv_a[0]), wq, gq128, wkt128, gkn128))
    u_s3 = u_s.reshape(nb, n_new, POOL_WIDTH)
    pool_t = _pool_sample(jnp.transpose(state_pool[0], (1, 0, 2)), jnp.transpose(u_s3, (1, 0, 2)), wpool, pscale)
    pool_s = jnp.transpose(pool_t, (1, 0, 2)).reshape(n_s, POOL_WIDTH)
    attn_s = _sample_attn(page_table, cache_ckv[0], cache_kpe[0], qlat_s, qf_s,
                          ckv_s.reshape(nb, n_new, KV_LORA), kpe_s.reshape(nb, n_new, ROPE_DIM),
                          tabs_attn, (wkt_flat, sel, gpe_b, wv)).reshape(n_s, ATTN_WIDTH)

    x1, h2, topi, gates, rank, counts = _post_attn(
        xmid_p, attn_p, xs_flat, attn_s, pool_s, (wout_a, wout_p, row(g_ffn[0]), wr_hi, wr_lo, br_b))
    nt = n_p + n_s
    counts = counts[:, 0]
    padded = (counts + TB - 1) // TB * TB
    ends = jnp.cumsum(padded)
    starts = ends - padded
    dest = (starts[topi] + rank).astype(jnp.int32)
    n_rows = nt * TOP_K + N_EXPERTS * TB
    n_blocks = n_rows // TB
    n_used = (ends[-1] // TB).astype(jnp.int32).reshape(1)
    blk_start = jnp.arange(n_blocks, dtype=jnp.int32) * TB
    blk_start = jnp.minimum(blk_start, jnp.maximum(ends[-1] - TB, 0))
    block_expert = jnp.minimum(jnp.searchsorted(ends, blk_start, side='right'), N_EXPERTS - 1).astype(jnp.int32)

    xs_sorted = _dispatch(dest, h2, jnp.zeros((n_rows, D_MODEL), F32))
    b3 = lambda v: v[0].reshape(N_EXPERTS, 1, D_MODEL).astype(F32)
    ys = _experts(block_expert, n_used, xs_sorted, w_gate[0].astype(BF16), b3(b_gate), w_up[0].astype(BF16),
                  b3(b_up), w_down[0].astype(BF16), b3(b_down))
    gates_t = gates.T
    fin_w = (row(g_ple[0]), w_ple_gate[0].astype(BF16), w_ple_proj[0].astype(BF16))
    y_p = _final(dest, x1, gates_t, p_prompt[0].reshape(n_p, PLE_DIM), ys, fin_w, 0, n_p)
    y_s = _final(dest, x1, gates_t, p_sample[0].reshape(n_s, PLE_DIM), ys, fin_w, n_p // TM_TOK, n_s)

    pool_state_s = jnp.concatenate([state_pool[0][:, n_new:], u_s3], axis=1)
    return (y_p.reshape(b, s, D_MODEL), y_s.reshape(nb, n_new, D_MODEL),
            ckv_p.reshape(1, b, s, KV_LORA), kpe_p.reshape(1, b, s, ROPE_DIM), pst_p[:, HALO - POOL_PREV:][None],
            ckv_s.reshape(1, nb, n_new, KV_LORA), kpe_s.reshape(1, nb, n_new, ROPE_DIM), pool_state_s[None])
```

```python
import functools

import jax
import jax.numpy as jnp
from jax import lax
from jax.experimental import pallas as pl
from jax.experimental.pallas import tpu as pltpu

F32 = jnp.float32
BF16 = jnp.bfloat16

D_MODEL = 1024
N_HEADS = 8
QK_NOPE = 64
ROPE_DIM = 32
ROPE_HALF = ROPE_DIM // 2
QK_HEAD = QK_NOPE + ROPE_DIM
V_HEAD = 64
Q_LORA = 384
KV_LORA = 256
ATTN_WIDTH = N_HEADS * V_HEAD
ROPE_THETA = 10000.0
SM_SCALE = QK_HEAD ** -0.5
POOL_WINDOWS = (2, 4, 8, 16)
POOL_GROUP_DIM = 128
POOL_WIDTH = 512
POOL_PREV = 15
N_EXPERTS = 32
TOP_K = 4
SWIGLU_LIMIT = 7.0
SWIGLU_ALPHA = 1.702
PLE_DIM = 256
EPS = 1e-6

LANES = 128
HALO = 16
MAIN_COLS = Q_LORA + KV_LORA + POOL_WIDTH + LANES
VMEM_LIMIT = 56 * 1024 * 1024

TM_PROJ = 256
TQ, TK = 256, 512
PAGES_PER_STEP = 8
SUB = 256
TM_TOK = 256
TM_DISP = 512
TB = 256

NEG_INF = float("-inf")


def _dot(a, b):
    return jnp.dot(a, b, preferred_element_type=F32)


def _dot_nt(a, b):
    return lax.dot_general(a, b, (((1,), (1,)), ((), ())), preferred_element_type=F32)


def _rms(x, n):
    return x * lax.rsqrt(jnp.sum(x * x, axis=-1, keepdims=True) * (1.0 / n) + EPS)


def _rope_rows(x, c_tab, sa_tab, sb_tab):
    return (x * c_tab + pltpu.roll(x, LANES - ROPE_HALF, 1) * sa_tab
            + pltpu.roll(x, ROPE_HALF, 1) * sb_tab)


def _proj_body(*refs, tm, seq_mode):
    if seq_mode:
        (x_ref, c_ref, sa_ref, sb_ref, gmix_ref, wmain_ref, gqa_ref, gkva_ref, wq_ref, gq_ref,
         wk_ref, wv_ref, gk_ref, wpool_ref, pscale_ref, woutp_ref,
         ckv_ref, kpe_ref, q_ref, k_ref, v_ref, xmid_ref, pst_ref, ubuf) = refs
    else:
        (x_ref, c_ref, sa_ref, sb_ref, gmix_ref, wmain_ref, gqa_ref, gkva_ref, wq_ref, gq_ref,
         wkt_ref, gkn_ref,
         ckv_ref, kpe_ref, u_ref, q_ref, qlat_ref) = refs

    x = x_ref[...]
    h = (_rms(x, D_MODEL) * gmix_ref[...]).astype(BF16)
    z = _dot(h, wmain_ref[...])
    cqn = (_rms(z[:, :Q_LORA], Q_LORA) * gqa_ref[...]).astype(BF16)
    ckvn = _rms(z[:, Q_LORA:Q_LORA + KV_LORA], KV_LORA) * gkva_ref[...]
    u = z[:, Q_LORA + KV_LORA:Q_LORA + KV_LORA + POOL_WIDTH]
    kpe128 = z[:, Q_LORA + KV_LORA + POOL_WIDTH:]
    ckv_ref[...] = ckvn
    kpe_ref[...] = pltpu.roll(kpe128, LANES - QK_NOPE, 1)[:, :ROPE_DIM]

    c_tab, sa_tab, sb_tab = c_ref[...], sa_ref[...], sb_ref[...]
    q_all = _dot(cqn, wq_ref[...])
    gq = gq_ref[...]
    for hd in range(N_HEADS):
        qh = q_all[:, hd * LANES:(hd + 1) * LANES]
        qh = _rope_rows(_rms(qh, QK_HEAD) * gq, c_tab, sa_tab, sb_tab) * SM_SCALE
        if seq_mode:
            q_ref[hd] = qh.astype(BF16)
        else:
            q_ref[hd] = qh
            qlat_ref[hd] = _dot((qh * gkn_ref[...]).astype(BF16), wkt_ref[hd])

    if not seq_mode:
        u_ref[...] = u
        return

    ckvb = ckvn.astype(BF16)
    k_all = _dot(ckvb, wk_ref[...])
    gk = gk_ref[...]
    for hd in range(N_HEADS):
        kh = k_all[:, hd * LANES:(hd + 1) * LANES] + kpe128
        kh = _rope_rows(_rms(kh, QK_HEAD) * gk, c_tab, sa_tab, sb_tab)
        k_ref[hd] = kh.astype(BF16)
    v_all = _dot(ckvb, wv_ref[...])
    for hp in range(N_HEADS // 2):
        v_ref[hp] = v_all[:, hp * LANES:(hp + 1) * LANES].astype(BF16)

    j = pl.program_id(1)

    @pl.when(j == 0)
    def _():
        ubuf[0:HALO, :] = jnp.zeros((HALO, POOL_WIDTH), F32)

    ubuf[HALO:HALO + tm, :] = u
    pos1 = (j * tm + 1 + lax.broadcasted_iota(jnp.int32, (tm, 1), 0)).astype(F32)
    pool_parts = []
    for g, w in enumerate(POOL_WINDOWS):
        sl = slice(g * POOL_GROUP_DIM, (g + 1) * POOL_GROUP_DIM)
        ug = u[:, sl]
        acc = ug
        for back in range(1, w):
            acc = acc + ubuf[HALO - back:HALO - back + tm, sl]
        d = acc / jnp.minimum(float(w), pos1) - ug
        pool_parts.append(_dot(d.astype(BF16), wpool_ref[g]) * pscale_ref[:, sl])
    pool_out = jnp.concatenate(pool_parts, axis=1).astype(BF16)
    xmid_ref[...] = x + _dot(pool_out, woutp_ref[...])

    tail = ubuf[tm:tm + HALO, :]
    ubuf[0:HALO, :] = tail

    @pl.when(j == pl.num_programs(1) - 1)
    def _():
        pst_ref[...] = tail


def _full_spec(shape):
    nd = len(shape)
    return pl.BlockSpec(shape, lambda *_: (0,) * nd)


def _proj_prompt(x, tabs, wts, tm=TM_PROJ):
    b, s, _ = x.shape
    n = b * s
    nj = s // tm
    c_tab, sa_tab, sb_tab = tabs
    row = lambda width: pl.BlockSpec((tm, width), lambda bi, j: (bi * nj + j, 0))
    tab = pl.BlockSpec((tm, LANES), lambda bi, j: (j, 0))
    heads = lambda nh, width: pl.BlockSpec((nh, tm, width), lambda bi, j: (0, bi * nj + j, 0))
    in_specs = [pl.BlockSpec((None, tm, D_MODEL), lambda bi, j: (bi, j, 0)), tab, tab, tab] + [
        _full_spec(w.shape) for w in wts]
    out_shape = (
        jax.ShapeDtypeStruct((n, KV_LORA), F32),
        jax.ShapeDtypeStruct((n, ROPE_DIM), F32),
        jax.ShapeDtypeStruct((N_HEADS, n, LANES), BF16),
        jax.ShapeDtypeStruct((N_HEADS, n, LANES), BF16),
        jax.ShapeDtypeStruct((N_HEADS // 2, n, LANES), BF16),
        jax.ShapeDtypeStruct((n, D_MODEL), F32),
        jax.ShapeDtypeStruct((b, HALO, POOL_WIDTH), F32),
    )
    out_specs = (row(KV_LORA), row(ROPE_DIM), heads(N_HEADS, LANES), heads(N_HEADS, LANES),
                 heads(N_HEADS // 2, LANES), row(D_MODEL),
                 pl.BlockSpec((None, HALO, POOL_WIDTH), lambda bi, j: (bi, 0, 0)))
    return pl.pallas_call(
        functools.partial(_proj_body, tm=tm, seq_mode=True),
        out_shape=out_shape, grid=(b, nj), in_specs=in_specs, out_specs=out_specs,
        scratch_shapes=[pltpu.VMEM((tm + HALO, POOL_WIDTH), F32)],
        compiler_params=pltpu.CompilerParams(dimension_semantics=("arbitrary", "arbitrary"),
                                             vmem_limit_bytes=VMEM_LIMIT),
        name="proj_prompt",
    )(x, c_tab, sa_tab, sb_tab, *wts)


def _proj_sample(x, tabs, wts, tm=TM_PROJ):
    n = x.shape[0]
    c_tab, sa_tab, sb_tab = tabs
    row = lambda width: pl.BlockSpec((tm, width), lambda i: (i, 0))
    heads = lambda width: pl.BlockSpec((N_HEADS, tm, width), lambda i: (0, i, 0))
    in_specs = [row(D_MODEL), _full_spec((tm, LANES)), _full_spec((tm, LANES)), _full_spec((tm, LANES))] + [
        _full_spec(w.shape) for w in wts]
    out_shape = (
        jax.ShapeDtypeStruct((n, KV_LORA), F32),
        jax.ShapeDtypeStruct((n, ROPE_DIM), F32),
        jax.ShapeDtypeStruct((n, POOL_WIDTH), F32),
        jax.ShapeDtypeStruct((N_HEADS, n, LANES), F32),
        jax.ShapeDtypeStruct((N_HEADS, n, KV_LORA), F32),
    )
    out_specs = (row(KV_LORA), row(ROPE_DIM), row(POOL_WIDTH), heads(LANES), heads(KV_LORA))
    return pl.pallas_call(
        functools.partial(_proj_body, tm=tm, seq_mode=False),
        out_shape=out_shape, grid=(n // tm,), in_specs=in_specs, out_specs=out_specs,
        compiler_params=pltpu.CompilerParams(dimension_semantics=("arbitrary",),
                                             vmem_limit_bytes=VMEM_LIMIT),
        name="proj_sample",
    )(x, c_tab, sa_tab, sb_tab, *wts)


def _flash_body(q_ref, k_ref, v_ref, o_ref, m_sc, l_sc, acc_sc, *, tq, tk):
    qi = pl.program_id(2)
    n_full = (qi * tq) // tk
    row = qi * tq + lax.broadcasted_iota(jnp.int32, (tq, tk), 0)
    col = lax.broadcasted_iota(jnp.int32, (tq, tk), 1)
    outs = []
    for hh in range(2):
        q = q_ref[hh]
        m_sc[...] = jnp.full((tq, LANES), NEG_INF, F32)
        l_sc[...] = jnp.zeros((tq, LANES), F32)
        acc_sc[...] = jnp.zeros((tq, LANES), F32)

        def step(ki, masked):
            start = pl.multiple_of(ki * tk, tk)
            s = _dot_nt(q, k_ref[hh, pl.ds(start, tk), :])
            if masked:
                s = jnp.where(col + start <= row, s, NEG_INF)
            m_prev = m_sc[:, :1]
            m_new = jnp.maximum(m_prev, jnp.max(s, axis=-1, keepdims=True))
            alpha = jnp.exp(m_prev - m_new)
            p = jnp.exp(s - m_new)
            l_sc[...] = jnp.broadcast_to(alpha * l_sc[:, :1] + jnp.sum(p, axis=-1, keepdims=True),
                                         (tq, LANES))
            acc_sc[...] = alpha * acc_sc[...] + _dot(p.astype(BF16), v_ref[pl.ds(start, tk), :])
            m_sc[...] = jnp.broadcast_to(m_new, (tq, LANES))

        def loop_body(ki, carry):
            step(ki, False)
            return carry

        lax.fori_loop(0, n_full, loop_body, 0)
        step(n_full, True)
        outs.append(acc_sc[...] / l_sc[:, :1])
    lane = lax.broadcasted_iota(jnp.int32, (tq, LANES), 1)
    o_ref[...] = jnp.where(lane < V_HEAD, outs[0], outs[1]).astype(o_ref.dtype)


def _flash_prompt(q, k, v, b, s, tq=TQ, tk=TK):
    n = b * s
    nq = s // tq
    return pl.pallas_call(
        functools.partial(_flash_body, tq=tq, tk=tk),
        out_shape=jax.ShapeDtypeStruct((n, ATTN_WIDTH), BF16),
        grid=(b, N_HEADS // 2, nq),
        in_specs=[
            pl.BlockSpec((2, tq, LANES), lambda bi, hp, qi: (hp, bi * nq + qi, 0)),
            pl.BlockSpec((2, s, LANES), lambda bi, hp, qi: (hp, bi, 0)),
            pl.BlockSpec((None, s, LANES), lambda bi, hp, qi: (hp, bi, 0)),
        ],
        out_specs=pl.BlockSpec((tq, LANES), lambda bi, hp, qi: (bi * nq + qi, hp)),
        scratch_shapes=[pltpu.VMEM((tq, LANES), F32)] * 3,
        compiler_params=pltpu.CompilerParams(dimension_semantics=("arbitrary",) * 3,
                                             vmem_limit_bytes=VMEM_LIMIT),
        name="flash_prompt",
    )(q, k, v)


def _pool_sample_body(st_ref, u_ref, wpool_ref, pscale_ref, o_ref, *, n_new):
    for t in range(n_new):
        for g, w in enumerate(POOL_WINDOWS):
            sl = slice(g * POOL_GROUP_DIM, (g + 1) * POOL_GROUP_DIM)
            ug = u_ref[t, :, sl]
            acc = ug
            for back in range(1, w):
                src = t - back
                acc = acc + (u_ref[src, :, sl] if src >= 0 else st_ref[POOL_PREV + src, :, sl])
            d = acc / float(w) - ug
            o_ref[t, :, sl] = _dot(d.astype(BF16), wpool_ref[g]) * pscale_ref[:, sl]


def _pool_sample(state_t, u_t, wpool, pscale):
    n_new, nb, _ = u_t.shape
    return pl.pallas_call(
        functools.partial(_pool_sample_body, n_new=n_new),
        out_shape=jax.ShapeDtypeStruct((n_new, nb, POOL_WIDTH), F32),
        grid=(1,),
        in_specs=[_full_spec(state_t.shape), _full_spec(u_t.shape), _full_spec(wpool.shape),
                  _full_spec(pscale.shape)],
        out_specs=_full_spec((n_new, nb, POOL_WIDTH)),
        compiler_params=pltpu.CompilerParams(dimension_semantics=("arbitrary",),
                                             vmem_limit_bytes=VMEM_LIMIT),
        name="pool_sample",
    )(state_t, u_t, wpool, pscale)


def _sample_attn_body(pt_ref, ckv_hbm, kpe_hbm, qlat_ref, q_ref, ckvn_ref, kpen_ref, cos_ref, sin_ref,
                      cost_ref, sint_ref, wkt_ref, gpe_ref, wv_ref, o_ref,
                      cbuf, kbuf, sem_c, sem_k, a_sc, m_sc, l_sc, acc_sc, *, n_pages, n_new):
    npg = PAGES_PER_STEP
    page = LANES
    n_chunks = n_pages // npg
    b = pl.program_id(0)
    nb = pl.num_programs(0)
    rows = N_HEADS * n_new

    def page_copies(seq, chunk, slot):
        cps = []
        for i in range(npg):
            pg = pt_ref[seq, chunk * npg + i]
            cps.append(pltpu.make_async_copy(ckv_hbm.at[pg], cbuf.at[slot, pl.ds(i * page, page)], sem_c.at[slot]))
            cps.append(pltpu.make_async_copy(kpe_hbm.at[pg], kbuf.at[slot, i], sem_k.at[slot]))
        return cps

    def start_chunk(seq, chunk, slot):
        for cp in page_copies(seq, chunk, slot):
            cp.start()

    def wait_chunk(slot):
        for cp in page_copies(0, 0, slot):
            cp.wait()

    @pl.when(b == 0)
    def _():
        start_chunk(0, 0, 0)

    a_sc[0:N_HEADS * QK_NOPE, :] = wkt_ref[...]
    a_sc[N_HEADS * QK_NOPE:, :] = qlat_ref[...].reshape(rows, KV_LORA).astype(BF16)
    m_sc[...] = jnp.full((rows, LANES), NEG_INF, F32)
    l_sc[...] = jnp.zeros((rows, LANES), F32)
    acc_sc[...] = jnp.zeros((rows, KV_LORA), F32)
    qpe = q_ref[...].reshape(rows, LANES)[:, QK_NOPE:QK_HEAD].astype(BF16)
    g1 = gpe_ref[0:ROPE_HALF, :]
    g2 = gpe_ref[ROPE_HALF:ROPE_DIM, :]

    def scores(cb, kpt, cos, sin):
        ck = cb.shape[0]
        r = _dot_nt(a_sc[...], cb)
        kt = r[:N_HEADS * QK_NOPE]
        ssn = jnp.sum((kt * kt).reshape(N_HEADS, QK_NOPE, ck), axis=1)
        x1 = kpt[0:ROPE_HALF]
        x2 = kpt[ROPE_HALF:ROPE_DIM]
        ssp = jnp.sum(x1 * x1 + x2 * x2, axis=0, keepdims=True)
        rs = lax.rsqrt((ssn + ssp) * (1.0 / QK_HEAD) + EPS)
        reps = ck // LANES
        x1 = x1 * jnp.tile(g1, (1, reps))
        x2 = x2 * jnp.tile(g2, (1, reps))
        rot = jnp.concatenate([x1 * cos - x2 * sin, x2 * cos + x1 * sin], axis=0).astype(BF16)
        s = r[N_HEADS * QK_NOPE:] + _dot(qpe, rot)
        s = s.reshape(N_HEADS, n_new, ck) * rs[:, None, :]
        return s.reshape(rows, ck)

    def update(s_all, cbs):
        m_prev = m_sc[:, :1]
        m_new = jnp.maximum(m_prev, jnp.max(s_all, axis=-1, keepdims=True))
        alpha = jnp.exp(m_prev - m_new)
        p = jnp.exp(s_all - m_new)
        l_sc[...] = jnp.broadcast_to(alpha * l_sc[:, :1] + jnp.sum(p, axis=-1, keepdims=True),
                                     (rows, LANES))
        acc = alpha * acc_sc[...]
        off = 0
        for cb in cbs:
            ck = cb.shape[0]
            acc = acc + _dot(p[:, off:off + ck].astype(BF16), cb)
            off += ck
        acc_sc[...] = acc
        m_sc[...] = jnp.broadcast_to(m_new, (rows, LANES))

    pages_per_sub = SUB // page
    total = nb * n_chunks

    def chunk(jc, slot):
        wait_chunk(slot)
        nxt = jnp.minimum(b * n_chunks + jc + 1, total - 1)
        start_chunk(nxt // n_chunks, nxt % n_chunks, 1 - slot)
        cbs, s_parts = [], []
        for sc in range(npg // pages_per_sub):
            cb = cbuf[slot, sc * SUB:(sc + 1) * SUB, :].astype(BF16)
            kpt = jnp.concatenate([kbuf[slot, sc * pages_per_sub + i] for i in range(pages_per_sub)], axis=1)
            start = pl.multiple_of(jc * (npg * page) + sc * SUB, SUB)
            cbs.append(cb)
            s_parts.append(scores(cb, kpt, cos_ref[:, pl.ds(start, SUB)], sin_ref[:, pl.ds(start, SUB)]))
        update(jnp.concatenate(s_parts, axis=1), cbs)

    def pair(jp, carry):
        chunk(2 * jp, 0)
        chunk(2 * jp + 1, 1)
        return carry

    lax.fori_loop(0, n_chunks // 2, pair, 0)

    @pl.when(b == nb - 1)
    def _():
        wait_chunk(0)

    pad = LANES - n_new
    cb = jnp.concatenate([ckvn_ref[...], jnp.zeros((pad, KV_LORA), F32)], axis=0).astype(BF16)
    s = scores(cb, kpen_ref[...], cost_ref[...], sint_ref[...])
    key = lax.broadcasted_iota(jnp.int32, (rows, LANES), 1)
    tok = lax.broadcasted_iota(jnp.int32, (rows, LANES), 0) % n_new
    update(jnp.where(key <= tok, s, NEG_INF), [cb])
    o_lat = (acc_sc[...] / l_sc[:, :1]).astype(BF16)
    full = _dot(o_lat, wv_ref[...]).reshape(N_HEADS, n_new, ATTN_WIDTH)
    hd = lax.broadcasted_iota(jnp.int32, (N_HEADS, n_new, ATTN_WIDTH), 0)
    colh = lax.broadcasted_iota(jnp.int32, (N_HEADS, n_new, ATTN_WIDTH), 2) // V_HEAD
    o_ref[...] = jnp.sum(jnp.where(hd == colh, full, 0.0), axis=0)


def _sample_attn(page_table, cache_ckv, cache_kpe_t, qlat, qf, ckv_new, kpe_new_t, tabs, wts):
    nb, n_pages = page_table.shape
    page = cache_ckv.shape[1]
    n_new = ckv_new.shape[1]
    npg = PAGES_PER_STEP
    assert page == LANES and n_pages % (2 * npg) == 0
    cos_t, sin_t, cos_tail, sin_tail = tabs
    rows = N_HEADS * n_new
    const = lambda shape: pl.BlockSpec(shape, lambda b, pt, nd=len(shape): (0,) * nd)
    in_specs = [
        pl.BlockSpec(memory_space=pl.ANY), pl.BlockSpec(memory_space=pl.ANY),
        pl.BlockSpec((N_HEADS, n_new, KV_LORA), lambda b, pt: (0, b, 0)),
        pl.BlockSpec((N_HEADS, n_new, LANES), lambda b, pt: (0, b, 0)),
        pl.BlockSpec((None, n_new, KV_LORA), lambda b, pt: (b, 0, 0)),
        pl.BlockSpec((None, ROPE_DIM, LANES), lambda b, pt: (b, 0, 0)),
        const(cos_t.shape), const(sin_t.shape), const(cos_tail.shape), const(sin_tail.shape),
    ] + [const(w.shape) for w in wts]
    grid_spec = pltpu.PrefetchScalarGridSpec(
        num_scalar_prefetch=1, grid=(nb,), in_specs=in_specs,
        out_specs=pl.BlockSpec((None, n_new, ATTN_WIDTH), lambda b, pt: (b, 0, 0)),
        scratch_shapes=[pltpu.VMEM((2, npg * page, KV_LORA), F32),
                        pltpu.VMEM((2, npg, ROPE_DIM, page), F32),
                        pltpu.SemaphoreType.DMA((2,)), pltpu.SemaphoreType.DMA((2,)),
                        pltpu.VMEM((N_HEADS * QK_NOPE + rows, KV_LORA), BF16),
                        pltpu.VMEM((rows, LANES), F32), pltpu.VMEM((rows, LANES), F32),
                        pltpu.VMEM((rows, KV_LORA), F32)])
    return pl.pallas_call(
        functools.partial(_sample_attn_body, n_pages=n_pages, n_new=n_new),
        out_shape=jax.ShapeDtypeStruct((nb, n_new, ATTN_WIDTH), F32),
        grid_spec=grid_spec,
        compiler_params=pltpu.CompilerParams(dimension_semantics=("arbitrary",),
                                             vmem_limit_bytes=VMEM_LIMIT),
        name="sample_attn",
    )(page_table, cache_ckv, cache_kpe_t, qlat, qf, ckv_new, kpe_new_t, cos_t, sin_t, cos_tail, sin_tail, *wts)


def _post_attn_body(xmid_ref, attn_p_ref, xs_ref, attn_s_ref, pool_s_ref, wouta_ref, woutp_ref, gffn_ref,
                    wrh_ref, wrl_ref, br_ref,
                    x1_ref, h2_ref, topi_ref, gate_ref, rank_ref, cnt_ref, base_sc, xm_sc, at_sc,
                    *, tm, n_prompt_tiles):
    i = pl.program_id(0)

    @pl.when(i == 0)
    def _():
        base_sc[...] = jnp.zeros((N_EXPERTS, LANES), F32)

    @pl.when(i < n_prompt_tiles)
    def _():
        xm_sc[...] = xmid_ref[...]
        at_sc[...] = attn_p_ref[...]

    @pl.when(i >= n_prompt_tiles)
    def _():
        xm_sc[...] = xs_ref[...] + _dot(pool_s_ref[...].astype(BF16), woutp_ref[...])
        at_sc[...] = attn_s_ref[...].astype(BF16)

    x1 = xm_sc[...] + _dot(at_sc[...], wouta_ref[...])
    x1_ref[...] = x1
    h2 = _rms(x1, D_MODEL) * gffn_ref[...]
    h2_ref[...] = h2
    h_hi = h2.astype(BF16)
    h_lo = (h2 - h_hi.astype(F32)).astype(BF16)
    lg = (_dot_nt(wrh_ref[...], h_hi) + _dot_nt(wrh_ref[...], h_lo) + _dot_nt(wrl_ref[...], h_hi)
          + br_ref[:, :1])
    eid = lax.broadcasted_iota(jnp.int32, (N_EXPERTS, tm), 0)
    vals, idxs, hots = [], [], []
    for _ in range(TOP_K):
        mx = jnp.max(lg, axis=0, keepdims=True)
        ix = jnp.min(jnp.where(lg == mx, eid, N_EXPERTS), axis=0, keepdims=True)
        hot = eid == ix
        lg = jnp.where(hot, NEG_INF, lg)
        vals.append(mx)
        idxs.append(ix)
        hots.append(hot)
    ex = [jnp.exp(v - vals[0]) for v in vals]
    den = ex[0] + ex[1] + ex[2] + ex[3]
    gate_ref[...] = jnp.concatenate([e / den for e in ex], axis=0)
    topi_ref[...] = jnp.concatenate(idxs, axis=0)

    anyhot = (hots[0] | hots[1] | hots[2] | hots[3])
    hot_f = jnp.where(anyhot, 1.0, 0.0)
    upper = jnp.where(lax.broadcasted_iota(jnp.int32, (tm, tm), 0) < lax.broadcasted_iota(jnp.int32, (tm, tm), 1),
                      1.0, 0.0).astype(BF16)
    before = _dot(hot_f.astype(BF16), upper) + base_sc[:, :1]
    ranks = [jnp.sum(jnp.where(h, before, 0.0), axis=0, keepdims=True) for h in hots]
    rank_ref[...] = jnp.concatenate(ranks, axis=0).astype(jnp.int32)
    base_sc[...] = base_sc[...] + jnp.sum(hot_f, axis=1, keepdims=True)
    cnt_ref[...] = base_sc[...].astype(jnp.int32)


def _post_attn(xmid_p, attn_p, x_s, attn_s, pool_s, wts, tm=TM_TOK):
    n_p, n_s = xmid_p.shape[0], x_s.shape[0]
    npt, nst = n_p // tm, n_s // tm
    nt = n_p + n_s
    p_idx = lambda i: (jnp.minimum(i, npt - 1), 0)
    s_idx = lambda i: (jnp.maximum(i - npt, 0), 0)
    in_specs = [
        pl.BlockSpec((tm, D_MODEL), p_idx), pl.BlockSpec((tm, ATTN_WIDTH), p_idx),
        pl.BlockSpec((tm, D_MODEL), s_idx), pl.BlockSpec((tm, ATTN_WIDTH), s_idx),
        pl.BlockSpec((tm, POOL_WIDTH), s_idx),
    ] + [_full_spec(w.shape) for w in wts]
    col = lambda rows: pl.BlockSpec((rows, tm), lambda i: (0, i))
    out_shape = (
        jax.ShapeDtypeStruct((nt, D_MODEL), F32), jax.ShapeDtypeStruct((nt, D_MODEL), F32),
        jax.ShapeDtypeStruct((TOP_K, nt), jnp.int32), jax.ShapeDtypeStruct((TOP_K, nt), F32),
        jax.ShapeDtypeStruct((TOP_K, nt), jnp.int32), jax.ShapeDtypeStruct((N_EXPERTS, LANES), jnp.int32),
    )
    out_specs = (pl.BlockSpec((tm, D_MODEL), lambda i: (i, 0)), pl.BlockSpec((tm, D_MODEL), lambda i: (i, 0)),
                 col(TOP_K), col(TOP_K), col(TOP_K), _full_spec((N_EXPERTS, LANES)))
    return pl.pallas_call(
        functools.partial(_post_attn_body, tm=tm, n_prompt_tiles=npt),
        out_shape=out_shape, grid=(npt + nst,), in_specs=in_specs, out_specs=out_specs,
        scratch_shapes=[pltpu.VMEM((N_EXPERTS, LANES), F32), pltpu.VMEM((tm, D_MODEL), F32),
                        pltpu.VMEM((tm, ATTN_WIDTH), BF16)],
        compiler_params=pltpu.CompilerParams(dimension_semantics=("arbitrary",),
                                             vmem_limit_bytes=VMEM_LIMIT),
        name="post_attn",
    )(xmid_p, attn_p, x_s, attn_s, pool_s, *wts)


def _invert_body(dest_ref, zeros_hbm, rt_ref, sem, *, tm):
    i = pl.program_id(0)

    @pl.when(i == 0)
    def _():
        cp = pltpu.make_async_copy(zeros_hbm, rt_ref, sem)
        cp.start()
        cp.wait()

    base = i * tm

    def body(t, carry):
        for k in range(TOP_K):
            rt_ref[dest_ref[k, t]] = base + t
        return carry

    lax.fori_loop(0, tm, body, 0)


def _invert(dest, n_rows, tm=TM_DISP):
    nt = dest.shape[1]
    return pl.pallas_call(
        functools.partial(_invert_body, tm=tm),
        out_shape=jax.ShapeDtypeStruct((n_rows,), jnp.int32),
        grid=(nt // tm,),
        in_specs=[pl.BlockSpec((TOP_K, tm), lambda i: (0, i), memory_space=pltpu.SMEM),
                  pl.BlockSpec(memory_space=pl.ANY)],
        out_specs=pl.BlockSpec(memory_space=pltpu.SMEM),
        scratch_shapes=[pltpu.SemaphoreType.DMA(())],
        compiler_params=pltpu.CompilerParams(dimension_semantics=("arbitrary",)),
        name="invert",
    )(dest, jnp.zeros((n_rows,), jnp.int32))


def _experts_body(be_ref, nu_ref, rt_cur_ref, rt_nxt_ref, h2_ref, wg_ref, bg_ref, wu_ref, bu_ref, wd_ref, bd_ref,
                  y_ref, xbuf, sem, *, tb):
    blk = pl.program_id(0)
    slot = blk % 2
    n_used = nu_ref[0]

    def row_copy(rt_ref, r, s):
        return pltpu.make_async_copy(h2_ref.at[pl.ds(rt_ref[r], 1)], xbuf.at[s, pl.ds(r, 1)], sem.at[s])

    def gather(rt_ref, s):
        def body(r, carry):
            row_copy(rt_ref, r, s).start()
            return carry
        lax.fori_loop(0, tb, body, 0)

    @pl.when((blk == 0) & (n_used > 0))
    def _():
        gather(rt_cur_ref, 0)

    @pl.when(blk + 1 < n_used)
    def _():
        gather(rt_nxt_ref, 1 - slot)

    @pl.when(blk < n_used)
    def _():
        def drain(r, carry):
            row_copy(rt_cur_ref, 0, slot).wait()
            return carry
        lax.fori_loop(0, tb, drain, 0)
        xb = xbuf[slot].astype(BF16)
        g = jnp.minimum(_dot(xb, wg_ref[...]) + bg_ref[...], SWIGLU_LIMIT)
        u = jnp.clip(_dot(xb, wu_ref[...]) + bu_ref[...], -SWIGLU_LIMIT, SWIGLU_LIMIT)
        act = (u + 1.0) * (g * jax.nn.sigmoid(SWIGLU_ALPHA * g))
        y_ref[...] = _dot(act.astype(BF16), wd_ref[...]) + bd_ref[...]

    @pl.when(blk >= n_used)
    def _():
        y_ref[...] = jnp.zeros(y_ref.shape, F32)


def _experts(block_expert, n_used, row_token, h2, wg, bg, wu, bu, wd, bd, tb=TB):
    r = row_token.shape[0]
    nblk = r // tb
    w_spec = pl.BlockSpec((None, D_MODEL, D_MODEL), lambda b, be, nu: (be[b], 0, 0))
    b_spec = pl.BlockSpec((None, 1, D_MODEL), lambda b, be, nu: (be[b], 0, 0))
    rt_cur = pl.BlockSpec((tb,), lambda b, be, nu: (b,), memory_space=pltpu.SMEM)
    rt_nxt = pl.BlockSpec((tb,), lambda b, be, nu: (jnp.minimum(b + 1, nblk - 1),), memory_space=pltpu.SMEM)
    grid_spec = pltpu.PrefetchScalarGridSpec(
        num_scalar_prefetch=2, grid=(nblk,),
        in_specs=[rt_cur, rt_nxt, pl.BlockSpec(memory_space=pl.ANY),
                  w_spec, b_spec, w_spec, b_spec, w_spec, b_spec],
        out_specs=pl.BlockSpec((tb, D_MODEL), lambda b, be, nu: (b, 0)),
        scratch_shapes=[pltpu.VMEM((2, tb, D_MODEL), F32), pltpu.SemaphoreType.DMA((2,))])
    return pl.pallas_call(
        functools.partial(_experts_body, tb=tb),
        out_shape=jax.ShapeDtypeStruct((r, D_MODEL), F32), grid_spec=grid_spec,
        compiler_params=pltpu.CompilerParams(dimension_semantics=("arbitrary",),
                                             vmem_limit_bytes=VMEM_LIMIT),
        name="experts",
    )(block_expert, n_used, row_token, row_token, h2, wg, bg, wu, bu, wd, bd)


def _final_body(dest_ref, x1_ref, gate_ref, p_ref, ys_ref, gple_ref, wpg_ref, wpp_ref, o_ref, gbuf, sem, *, tm):
    def issue(t, carry):
        for k in range(TOP_K):
            pltpu.make_async_copy(ys_ref.at[pl.ds(dest_ref[k, t], 1)], gbuf.at[k, pl.ds(t, 1)], sem).start()
        return carry

    lax.fori_loop(0, tm, issue, 0)
    pp = _dot(p_ref[...].astype(BF16), wpp_ref[...])

    def drain(t, carry):
        for k in range(TOP_K):
            pltpu.make_async_copy(ys_ref.at[pl.ds(0, 1)], gbuf.at[0, pl.ds(0, 1)], sem).wait()
        return carry

    lax.fori_loop(0, tm, drain, 0)
    gates = gate_ref[...]
    x2 = x1_ref[...]
    for k in range(TOP_K):
        x2 = x2 + gates[:, k:k + 1] * gbuf[k]
    hn = (_rms(x2, D_MODEL) * gple_ref[...]).astype(BF16)
    o_ref[...] = x2 + jax.nn.sigmoid(_dot(hn, wpg_ref[...])) * pp


def _final(dest, x1, gates_t, p, ys, wts, tile_off, n_tok, tm=TM_TOK):
    in_specs = [
        pl.BlockSpec((TOP_K, tm), lambda i: (0, i + tile_off), memory_space=pltpu.SMEM),
        pl.BlockSpec((tm, D_MODEL), lambda i: (i + tile_off, 0)),
        pl.BlockSpec((tm, TOP_K), lambda i: (i + tile_off, 0)),
        pl.BlockSpec((tm, PLE_DIM), lambda i: (i, 0)),
        pl.BlockSpec(memory_space=pl.ANY),
    ] + [_full_spec(w.shape) for w in wts]
    return pl.pallas_call(
        functools.partial(_final_body, tm=tm),
        out_shape=jax.ShapeDtypeStruct((n_tok, D_MODEL), F32),
        grid=(n_tok // tm,), in_specs=in_specs,
        out_specs=pl.BlockSpec((tm, D_MODEL), lambda i: (i, 0)),
        scratch_shapes=[pltpu.VMEM((TOP_K, tm, D_MODEL), F32), pltpu.SemaphoreType.DMA(())],
        compiler_params=pltpu.CompilerParams(dimension_semantics=("arbitrary",),
                                             vmem_limit_bytes=VMEM_LIMIT),
        name="final",
    )(dest, x1, gates_t, p, ys, *wts)


def _rope_angles(pos):
    inv = ROPE_THETA ** (-jnp.arange(ROPE_HALF, dtype=F32) / ROPE_HALF)
    ang = pos.astype(F32)[:, None] * inv
    return jnp.cos(ang), jnp.sin(ang)


def _rope_row_tables(pos):
    cos, sin = _rope_angles(pos)
    n = pos.shape[0]
    z = lambda w: jnp.zeros((n, w), F32)
    c_tab = jnp.concatenate([jnp.ones((n, QK_NOPE), F32), cos, cos, z(LANES - QK_HEAD)], axis=1)
    sa_tab = jnp.concatenate([z(QK_NOPE), -sin, z(ROPE_HALF), z(LANES - QK_HEAD)], axis=1)
    sb_tab = jnp.concatenate([z(QK_NOPE), z(ROPE_HALF), sin, z(LANES - QK_HEAD)], axis=1)
    return c_tab, sa_tab, sb_tab


def _pad_lanes(v, width=LANES):
    return jnp.pad(v, [(0, 0)] * (v.ndim - 1) + [(0, width - v.shape[-1])])


def kernel(x_prompt, x_sample, cache_ckv, cache_kpe, state_pool, page_table, p_prompt, p_sample, g_mix, w_in, g_q_a, w_q_b, g_kv_a, w_kv_b, g_q_head, g_k_head, w_pool, pool_scale, w_out, g_ffn, w_router, b_router, w_gate, b_gate, w_up, b_up, w_down, b_down, g_ple, w_ple_gate, w_ple_proj):
    assert x_prompt.shape[-1] == D_MODEL and g_mix.shape[0] == 1
    b, s, _ = x_prompt.shape
    nb, n_new, _ = x_sample.shape
    n_p, n_s = b * s, nb * n_new
    past = page_table.shape[1] * cache_ckv.shape[2]
    row = lambda v: v.reshape(1, -1).astype(F32)

    wi = w_in[0]
    w_cq, w_ckv = wi[:, :Q_LORA], wi[:, Q_LORA:Q_LORA + KV_LORA]
    w_kpe = wi[:, Q_LORA + KV_LORA:Q_LORA + KV_LORA + ROPE_DIM]
    w_u = wi[:, Q_LORA + KV_LORA + ROPE_DIM:]
    w_kpe128 = jnp.pad(w_kpe, ((0, 0), (QK_NOPE, LANES - QK_HEAD)))
    w_main = jnp.concatenate([w_cq, w_ckv, w_u, w_kpe128], axis=1).astype(BF16)
    wq = _pad_lanes(w_q_b[0]).reshape(Q_LORA, N_HEADS * LANES).astype(BF16)
    wk_nope = w_kv_b[0][:, :, :QK_NOPE]
    wk = _pad_lanes(wk_nope).reshape(KV_LORA, N_HEADS * LANES).astype(BF16)
    wv = w_kv_b[0][:, :, QK_NOPE:].reshape(KV_LORA, ATTN_WIDTH).astype(BF16)
    gq128 = _pad_lanes(row(g_q_head[0]))
    gk128 = _pad_lanes(row(g_k_head[0]))
    gkn128 = _pad_lanes(row(g_k_head[0][:QK_NOPE]))
    wkt = jnp.transpose(wk_nope, (1, 2, 0))
    wkt128 = jnp.pad(wkt, ((0, 0), (0, LANES - QK_NOPE), (0, 0))).astype(BF16)
    wkt_flat = wkt.reshape(N_HEADS * QK_NOPE, KV_LORA).astype(BF16)
    gpe_b = jnp.broadcast_to(g_k_head[0][QK_NOPE:].astype(F32)[:, None], (ROPE_DIM, LANES))
    wpool = w_pool[0].astype(BF16)
    pscale = row(pool_scale[0])
    wout_a = w_out[0][:ATTN_WIDTH].astype(BF16)
    wout_p = w_out[0][ATTN_WIDTH:].astype(BF16)
    wr_t = w_router[0].T.astype(F32)
    wr_hi = wr_t.astype(BF16)
    wr_lo = (wr_t - wr_hi.astype(F32)).astype(BF16)
    br_b = jnp.broadcast_to(b_router[0].astype(F32)[:, None], (N_EXPERTS, LANES))

    tabs_p = _rope_row_tables(jnp.arange(s, dtype=jnp.int32))
    pos_s = past + (jnp.arange(TM_PROJ, dtype=jnp.int32) % n_new)
    tabs_s = _rope_row_tables(pos_s)
    cos_t, sin_t = (t.T for t in _rope_angles(jnp.arange(past + LANES, dtype=jnp.int32)))
    tabs_attn = (cos_t[:, :past], sin_t[:, :past], cos_t[:, past:], sin_t[:, past:])

    ckv_p, kpe_p, q_p, k_p, v_p, xmid_p, pst_p = _proj_prompt(
        x_prompt, tabs_p,
        (row(g_mix[0]), w_main, row(g_q_a[0]), row(g_kv_a[0]), wq, gq128, wk, wv, gk128, wpool, pscale, wout_p))
    attn_p = _flash_prompt(q_p, k_p, v_p, b, s)

    xs_flat = x_sample.reshape(n_s, D_MODEL)
    ckv_s, kpe_s, u_s, qf_s, qlat_s = _proj_sample(
        xs_flat, tabs_s, (row(g_mix[0]), w_main, row(g_q_a[0]), row(g_kv_a[0]), wq, gq128, wkt128, gkn128))
    u_s3 = u_s.reshape(nb, n_new, POOL_WIDTH)
    pool_t = _pool_sample(jnp.transpose(state_pool[0], (1, 0, 2)), jnp.transpose(u_s3, (1, 0, 2)), wpool, pscale)
    pool_s = jnp.transpose(pool_t, (1, 0, 2)).reshape(n_s, POOL_WIDTH)
    kpe_new_t = _pad_lanes(jnp.transpose(kpe_s.reshape(nb, n_new, ROPE_DIM), (0, 2, 1)))
    attn_s = _sample_attn(page_table, cache_ckv[0], jnp.transpose(cache_kpe[0], (0, 2, 1)), qlat_s, qf_s,
                          ckv_s.reshape(nb, n_new, KV_LORA), kpe_new_t,
                          tabs_attn, (wkt_flat, gpe_b, wv)).reshape(n_s, ATTN_WIDTH)

    x1, h2, topi, gates, rank, counts = _post_attn(
        xmid_p, attn_p, xs_flat, attn_s, pool_s, (wout_a, wout_p, row(g_ffn[0]), wr_hi, wr_lo, br_b))
    nt = n_p + n_s
    counts = counts[:, 0]
    padded = (counts + TB - 1) // TB * TB
    ends = jnp.cumsum(padded)
    starts = ends - padded
    eids = jnp.arange(N_EXPERTS, dtype=jnp.int32)
    dest = (jnp.sum(jnp.where(topi[..., None] == eids, starts, 0), axis=-1) + rank).astype(jnp.int32)
    n_rows = nt * TOP_K + N_EXPERTS * TB
    n_blocks = n_rows // TB
    n_used = (ends[-1] // TB).astype(jnp.int32).reshape(1)
    blk_start = jnp.arange(n_blocks, dtype=jnp.int32) * TB
    blk_start = jnp.minimum(blk_start, jnp.maximum(ends[-1] - TB, 0))
    block_expert = jnp.minimum(jnp.sum((ends[None, :] <= blk_start[:, None]).astype(jnp.int32), axis=1),
                               N_EXPERTS - 1).astype(jnp.int32)

    row_token = _invert(dest, n_rows)
    b3 = lambda v: v[0].reshape(N_EXPERTS, 1, D_MODEL).astype(F32)
    ys = _experts(block_expert, n_used, row_token, h2, w_gate[0].astype(BF16), b3(b_gate), w_up[0].astype(BF16),
                  b3(b_up), w_down[0].astype(BF16), b3(b_down))
    gates_t = gates.T
    fin_w = (row(g_ple[0]), w_ple_gate[0].astype(BF16), w_ple_proj[0].astype(BF16))
    y_p = _final(dest, x1, gates_t, p_prompt[0].reshape(n_p, PLE_DIM), ys, fin_w, 0, n_p)
    y_s = _final(dest, x1, gates_t, p_sample[0].reshape(n_s, PLE_DIM), ys, fin_w, n_p // TM_TOK, n_s)

    pool_state_s = jnp.concatenate([state_pool[0][:, n_new:], u_s3], axis=1)
    return (y_p.reshape(b, s, D_MODEL), y_s.reshape(nb, n_new, D_MODEL),
            ckv_p.reshape(1, b, s, KV_LORA), kpe_p.reshape(1, b, s, ROPE_DIM), pst_p[:, HALO - POOL_PREV:][None],
            ckv_s.reshape(1, nb, n_new, KV_LORA), kpe_s.reshape(1, nb, n_new, ROPE_DIM), pool_state_s[None])
```

```python
import functools

import jax
import jax.numpy as jnp
from jax import lax
from jax.experimental import pallas as pl
from jax.experimental.pallas import tpu as pltpu

F32 = jnp.float32
BF16 = jnp.bfloat16

D_MODEL = 1024
N_HEADS = 8
QK_NOPE = 64
ROPE_DIM = 32
ROPE_HALF = ROPE_DIM // 2
QK_HEAD = QK_NOPE + ROPE_DIM
V_HEAD = 64
Q_LORA = 384
KV_LORA = 256
ATTN_WIDTH = N_HEADS * V_HEAD
ROPE_THETA = 10000.0
SM_SCALE = QK_HEAD ** -0.5
POOL_WINDOWS = (2, 4, 8, 16)
POOL_GROUP_DIM = 128
POOL_WIDTH = 512
POOL_PREV = 15
N_EXPERTS = 32
TOP_K = 4
SWIGLU_LIMIT = 7.0
SWIGLU_ALPHA = 1.702
PLE_DIM = 256
EPS = 1e-6

LANES = 128
HALO = 16
MAIN_COLS = Q_LORA + KV_LORA + POOL_WIDTH + LANES
VMEM_LIMIT = 56 * 1024 * 1024

TM_PROJ = 256
TQ, TK = 256, 512
PAGES_PER_STEP = 16
RING = 3
SUB = 256
TM_TOK = 256
TM_DISP = 512
TB = 256
GATHER_UNROLL = 8

NEG_INF = float("-inf")


def _dot(a, b):
    return jnp.dot(a, b, preferred_element_type=F32)


def _dot_nt(a, b):
    return lax.dot_general(a, b, (((1,), (1,)), ((), ())), preferred_element_type=F32)


def _rms(x, n):
    return x * lax.rsqrt(jnp.sum(x * x, axis=-1, keepdims=True) * (1.0 / n) + EPS)


def _rope_rows(x, c_tab, sa_tab, sb_tab):
    return (x * c_tab + pltpu.roll(x, LANES - ROPE_HALF, 1) * sa_tab
            + pltpu.roll(x, ROPE_HALF, 1) * sb_tab)


def _proj_body(*refs, tm, seq_mode):
    if seq_mode:
        (x_ref, c_ref, sa_ref, sb_ref, gmix_ref, wmain_ref, gqa_ref, gkva_ref, wq_ref, gq_ref,
         wk_ref, wv_ref, gk_ref, wpool_ref, pscale_ref, woutp_ref,
         ckv_ref, kpe_ref, q_ref, k_ref, v_ref, xmid_ref, pst_ref, ubuf) = refs
    else:
        (x_ref, c_ref, sa_ref, sb_ref, gmix_ref, wmain_ref, gqa_ref, gkva_ref, wq_ref, gq_ref,
         wkt_ref, gkn_ref,
         ckv_ref, kpe_ref, u_ref, q_ref, qlat_ref) = refs

    x = x_ref[...]
    h = (_rms(x, D_MODEL) * gmix_ref[...]).astype(BF16)
    z = _dot(h, wmain_ref[...])
    cqn = (_rms(z[:, :Q_LORA], Q_LORA) * gqa_ref[...]).astype(BF16)
    ckvn = _rms(z[:, Q_LORA:Q_LORA + KV_LORA], KV_LORA) * gkva_ref[...]
    u = z[:, Q_LORA + KV_LORA:Q_LORA + KV_LORA + POOL_WIDTH]
    kpe128 = z[:, Q_LORA + KV_LORA + POOL_WIDTH:]
    ckv_ref[...] = ckvn
    kpe_ref[...] = pltpu.roll(kpe128, LANES - QK_NOPE, 1)[:, :ROPE_DIM]

    c_tab, sa_tab, sb_tab = c_ref[...], sa_ref[...], sb_ref[...]
    q_all = _dot(cqn, wq_ref[...])
    gq = gq_ref[...]
    for hd in range(N_HEADS):
        qh = q_all[:, hd * LANES:(hd + 1) * LANES]
        qh = _rope_rows(_rms(qh, QK_HEAD) * gq, c_tab, sa_tab, sb_tab) * SM_SCALE
        if seq_mode:
            q_ref[hd] = qh.astype(BF16)
        else:
            q_ref[hd] = qh
            qlat_ref[hd] = _dot((qh * gkn_ref[...]).astype(BF16), wkt_ref[hd])

    if not seq_mode:
        u_ref[...] = u
        return

    ckvb = ckvn.astype(BF16)
    k_all = _dot(ckvb, wk_ref[...])
    gk = gk_ref[...]
    for hd in range(N_HEADS):
        kh = k_all[:, hd * LANES:(hd + 1) * LANES] + kpe128
        kh = _rope_rows(_rms(kh, QK_HEAD) * gk, c_tab, sa_tab, sb_tab)
        k_ref[hd] = kh.astype(BF16)
    v_all = _dot(ckvb, wv_ref[...])
    for hp in range(N_HEADS // 2):
        v_ref[hp] = v_all[:, hp * LANES:(hp + 1) * LANES].astype(BF16)

    j = pl.program_id(1)

    @pl.when(j == 0)
    def _():
        ubuf[0:HALO, :] = jnp.zeros((HALO, POOL_WIDTH), F32)

    ubuf[HALO:HALO + tm, :] = u
    pos1 = (j * tm + 1 + lax.broadcasted_iota(jnp.int32, (tm, 1), 0)).astype(F32)
    pool_parts = []
    for g, w in enumerate(POOL_WINDOWS):
        sl = slice(g * POOL_GROUP_DIM, (g + 1) * POOL_GROUP_DIM)
        ug = u[:, sl]
        acc = ug
        for back in range(1, w):
            acc = acc + ubuf[HALO - back:HALO - back + tm, sl]
        d = acc / jnp.minimum(float(w), pos1) - ug
        pool_parts.append(_dot(d.astype(BF16), wpool_ref[g]) * pscale_ref[:, sl])
    pool_out = jnp.concatenate(pool_parts, axis=1).astype(BF16)
    xmid_ref[...] = x + _dot(pool_out, woutp_ref[...])

    tail = ubuf[tm:tm + HALO, :]
    ubuf[0:HALO, :] = tail

    @pl.when(j == pl.num_programs(1) - 1)
    def _():
        pst_ref[...] = tail


def _full_spec(shape):
    nd = len(shape)
    return pl.BlockSpec(shape, lambda *_: (0,) * nd)


def _proj_prompt(x, tabs, wts, tm=TM_PROJ):
    b, s, _ = x.shape
    n = b * s
    nj = s // tm
    c_tab, sa_tab, sb_tab = tabs
    row = lambda width: pl.BlockSpec((tm, width), lambda bi, j: (bi * nj + j, 0))
    tab = pl.BlockSpec((tm, LANES), lambda bi, j: (j, 0))
    heads = lambda nh, width: pl.BlockSpec((nh, tm, width), lambda bi, j: (0, bi * nj + j, 0))
    in_specs = [pl.BlockSpec((None, tm, D_MODEL), lambda bi, j: (bi, j, 0)), tab, tab, tab] + [
        _full_spec(w.shape) for w in wts]
    out_shape = (
        jax.ShapeDtypeStruct((n, KV_LORA), F32),
        jax.ShapeDtypeStruct((n, ROPE_DIM), F32),
        jax.ShapeDtypeStruct((N_HEADS, n, LANES), BF16),
        jax.ShapeDtypeStruct((N_HEADS, n, LANES), BF16),
        jax.ShapeDtypeStruct((N_HEADS // 2, n, LANES), BF16),
        jax.ShapeDtypeStruct((n, D_MODEL), F32),
        jax.ShapeDtypeStruct((b, HALO, POOL_WIDTH), F32),
    )
    out_specs = (row(KV_LORA), row(ROPE_DIM), heads(N_HEADS, LANES), heads(N_HEADS, LANES),
                 heads(N_HEADS // 2, LANES), row(D_MODEL),
                 pl.BlockSpec((None, HALO, POOL_WIDTH), lambda bi, j: (bi, 0, 0)))
    return pl.pallas_call(
        functools.partial(_proj_body, tm=tm, seq_mode=True),
        out_shape=out_shape, grid=(b, nj), in_specs=in_specs, out_specs=out_specs,
        scratch_shapes=[pltpu.VMEM((tm + HALO, POOL_WIDTH), F32)],
        compiler_params=pltpu.CompilerParams(dimension_semantics=("arbitrary", "arbitrary"),
                                             vmem_limit_bytes=VMEM_LIMIT),
        name="proj_prompt",
    )(x, c_tab, sa_tab, sb_tab, *wts)


def _proj_sample(x, tabs, wts, tm=TM_PROJ):
    n = x.shape[0]
    c_tab, sa_tab, sb_tab = tabs
    row = lambda width: pl.BlockSpec((tm, width), lambda i: (i, 0))
    heads = lambda width: pl.BlockSpec((N_HEADS, tm, width), lambda i: (0, i, 0))
    in_specs = [row(D_MODEL), _full_spec((tm, LANES)), _full_spec((tm, LANES)), _full_spec((tm, LANES))] + [
        _full_spec(w.shape) for w in wts]
    out_shape = (
        jax.ShapeDtypeStruct((n, KV_LORA), F32),
        jax.ShapeDtypeStruct((n, ROPE_DIM), F32),
        jax.ShapeDtypeStruct((n, POOL_WIDTH), F32),
        jax.ShapeDtypeStruct((N_HEADS, n, LANES), F32),
        jax.ShapeDtypeStruct((N_HEADS, n, KV_LORA), F32),
    )
    out_specs = (row(KV_LORA), row(ROPE_DIM), row(POOL_WIDTH), heads(LANES), heads(KV_LORA))
    return pl.pallas_call(
        functools.partial(_proj_body, tm=tm, seq_mode=False),
        out_shape=out_shape, grid=(n // tm,), in_specs=in_specs, out_specs=out_specs,
        compiler_params=pltpu.CompilerParams(dimension_semantics=("arbitrary",),
                                             vmem_limit_bytes=VMEM_LIMIT),
        name="proj_sample",
    )(x, c_tab, sa_tab, sb_tab, *wts)


def _flash_body(q_ref, k_ref, v_ref, o_ref, m_sc, l_sc, acc_sc, *, tq, tk):
    qi = pl.program_id(2)
    n_full = (qi * tq) // tk
    row = qi * tq + lax.broadcasted_iota(jnp.int32, (tq, tk), 0)
    col = lax.broadcasted_iota(jnp.int32, (tq, tk), 1)
    m_sc[...] = jnp.full(m_sc.shape, NEG_INF, F32)
    l_sc[...] = jnp.zeros(l_sc.shape, F32)
    acc_sc[...] = jnp.zeros(acc_sc.shape, F32)

    def step(ki, masked):
        start = pl.multiple_of(ki * tk, tk)
        v = v_ref[pl.ds(start, tk), :]
        for hh in range(2):
            s = _dot_nt(q_ref[hh], k_ref[hh, pl.ds(start, tk), :])
            if masked:
                s = jnp.where(col + start <= row, s, NEG_INF)
            m_prev = m_sc[hh]
            m_new = jnp.maximum(m_prev, jnp.max(s, axis=-1, keepdims=True))
            alpha = jnp.exp(m_prev - m_new)
            p = jnp.exp(s - m_new)
            l_sc[hh] = alpha * l_sc[hh] + jnp.sum(p, axis=-1, keepdims=True)
            acc_sc[hh] = alpha * acc_sc[hh] + _dot(p.astype(BF16), v)
            m_sc[hh] = m_new

    def loop_body(ki, carry):
        step(ki, False)
        return carry

    lax.fori_loop(0, n_full, loop_body, 0)
    step(n_full, True)
    lane = lax.broadcasted_iota(jnp.int32, (tq, LANES), 1)
    o_ref[...] = jnp.where(lane < V_HEAD, acc_sc[0] / l_sc[0], acc_sc[1] / l_sc[1]).astype(o_ref.dtype)


def _flash_prompt(q, k, v, b, s, tq=TQ, tk=TK):
    n = b * s
    nq = s // tq
    return pl.pallas_call(
        functools.partial(_flash_body, tq=tq, tk=tk),
        out_shape=jax.ShapeDtypeStruct((n, ATTN_WIDTH), BF16),
        grid=(b, N_HEADS // 2, nq),
        in_specs=[
            pl.BlockSpec((2, tq, LANES), lambda bi, hp, qi: (hp, bi * nq + qi, 0)),
            pl.BlockSpec((2, s, LANES), lambda bi, hp, qi: (hp, bi, 0)),
            pl.BlockSpec((None, s, LANES), lambda bi, hp, qi: (hp, bi, 0)),
        ],
        out_specs=pl.BlockSpec((tq, LANES), lambda bi, hp, qi: (bi * nq + qi, hp)),
        scratch_shapes=[pltpu.VMEM((2, tq, 1), F32), pltpu.VMEM((2, tq, 1), F32), pltpu.VMEM((2, tq, LANES), F32)],
        compiler_params=pltpu.CompilerParams(dimension_semantics=("arbitrary",) * 3,
                                             vmem_limit_bytes=VMEM_LIMIT),
        name="flash_prompt",
    )(q, k, v)


def _pool_sample_body(st_ref, u_ref, wpool_ref, pscale_ref, o_ref, *, n_new):
    for t in range(n_new):
        for g, w in enumerate(POOL_WINDOWS):
            sl = slice(g * POOL_GROUP_DIM, (g + 1) * POOL_GROUP_DIM)
            ug = u_ref[t, :, sl]
            acc = ug
            for back in range(1, w):
                src = t - back
                acc = acc + (u_ref[src, :, sl] if src >= 0 else st_ref[POOL_PREV + src, :, sl])
            d = acc / float(w) - ug
            o_ref[t, :, sl] = _dot(d.astype(BF16), wpool_ref[g]) * pscale_ref[:, sl]


def _pool_sample(state_t, u_t, wpool, pscale):
    n_new, nb, _ = u_t.shape
    return pl.pallas_call(
        functools.partial(_pool_sample_body, n_new=n_new),
        out_shape=jax.ShapeDtypeStruct((n_new, nb, POOL_WIDTH), F32),
        grid=(1,),
        in_specs=[_full_spec(state_t.shape), _full_spec(u_t.shape), _full_spec(wpool.shape),
                  _full_spec(pscale.shape)],
        out_specs=_full_spec((n_new, nb, POOL_WIDTH)),
        compiler_params=pltpu.CompilerParams(dimension_semantics=("arbitrary",),
                                             vmem_limit_bytes=VMEM_LIMIT),
        name="pool_sample",
    )(state_t, u_t, wpool, pscale)


def _sample_attn_body(pt_ref, ckv_hbm, kpe_hbm, qlat_ref, q_ref, ckvn_ref, kpen_ref, cos_ref, sin_ref,
                      cost_ref, sint_ref, wkt_ref, gpe_ref, wv_ref, o_ref,
                      cbuf, kbuf, sem_c, sem_k, a_sc, m_sc, l_sc, acc_sc, *, n_pages, n_new):
    npg = PAGES_PER_STEP
    page = LANES
    n_chunks = n_pages // npg
    b = pl.program_id(0)
    nb = pl.num_programs(0)
    rows = N_HEADS * n_new

    def page_copies(seq, chunk, slot):
        cps = []
        for i in range(npg):
            pg = pt_ref[seq, chunk * npg + i]
            cps.append(pltpu.make_async_copy(ckv_hbm.at[pg], cbuf.at[slot, pl.ds(i * page, page)], sem_c.at[slot]))
            cps.append(pltpu.make_async_copy(kpe_hbm.at[pg], kbuf.at[slot, i], sem_k.at[slot]))
        return cps

    def start_chunk(seq, chunk, slot):
        for cp in page_copies(seq, chunk, slot):
            cp.start()

    def wait_chunk(slot):
        for cp in page_copies(0, 0, slot):
            cp.wait()

    total = nb * n_chunks

    @pl.when(b == 0)
    def _():
        for g0 in range(RING - 1):
            start_chunk(g0 // n_chunks, g0 % n_chunks, g0 % RING)

    a_sc[0:N_HEADS * QK_NOPE, :] = wkt_ref[...]
    a_sc[N_HEADS * QK_NOPE:, :] = qlat_ref[...].reshape(rows, KV_LORA).astype(BF16)
    m_sc[...] = jnp.full((rows, LANES), NEG_INF, F32)
    l_sc[...] = jnp.zeros((rows, LANES), F32)
    acc_sc[...] = jnp.zeros((rows, KV_LORA), F32)
    qpe = q_ref[...].reshape(rows, LANES)[:, QK_NOPE:QK_HEAD].astype(BF16)
    g1 = gpe_ref[0:ROPE_HALF, :]
    g2 = gpe_ref[ROPE_HALF:ROPE_DIM, :]

    def scores(cb, kpt, cos, sin):
        ck = cb.shape[0]
        r = _dot_nt(a_sc[...], cb)
        kt = r[:N_HEADS * QK_NOPE]
        ssn = jnp.sum((kt * kt).reshape(N_HEADS, QK_NOPE, ck), axis=1)
        x1 = kpt[0:ROPE_HALF]
        x2 = kpt[ROPE_HALF:ROPE_DIM]
        ssp = jnp.sum(x1 * x1 + x2 * x2, axis=0, keepdims=True)
        rs = lax.rsqrt((ssn + ssp) * (1.0 / QK_HEAD) + EPS)
        reps = ck // LANES
        x1 = x1 * jnp.tile(g1, (1, reps))
        x2 = x2 * jnp.tile(g2, (1, reps))
        rot = jnp.concatenate([x1 * cos - x2 * sin, x2 * cos + x1 * sin], axis=0).astype(BF16)
        s = r[N_HEADS * QK_NOPE:] + _dot(qpe, rot)
        s = s.reshape(N_HEADS, n_new, ck) * rs[:, None, :]
        return s.reshape(rows, ck)

    def update(s_all, cbs):
        m_prev = m_sc[:, :1]
        m_new = jnp.maximum(m_prev, jnp.max(s_all, axis=-1, keepdims=True))
        alpha = jnp.exp(m_prev - m_new)
        p = jnp.exp(s_all - m_new)
        l_sc[...] = jnp.broadcast_to(alpha * l_sc[:, :1] + jnp.sum(p, axis=-1, keepdims=True),
                                     (rows, LANES))
        acc = alpha * acc_sc[...]
        off = 0
        for cb in cbs:
            ck = cb.shape[0]
            acc = acc + _dot(p[:, off:off + ck].astype(BF16), cb)
            off += ck
        acc_sc[...] = acc
        m_sc[...] = jnp.broadcast_to(m_new, (rows, LANES))

    pages_per_sub = SUB // page

    def chunk(jc, carry):
        g = b * n_chunks + jc
        slot = g % RING
        wait_chunk(slot)
        nxt = jnp.minimum(g + RING - 1, total - 1)
        start_chunk(nxt // n_chunks, nxt % n_chunks, (g + RING - 1) % RING)
        cbs, s_parts = [], []
        for sc in range(npg // pages_per_sub):
            cb = cbuf[slot, sc * SUB:(sc + 1) * SUB, :].astype(BF16)
            kpt = jnp.concatenate([kbuf[slot, sc * pages_per_sub + i] for i in range(pages_per_sub)], axis=1)
            start = pl.multiple_of(jc * (npg * page) + sc * SUB, SUB)
            cbs.append(cb)
            s_parts.append(scores(cb, kpt, cos_ref[:, pl.ds(start, SUB)], sin_ref[:, pl.ds(start, SUB)]))
        update(jnp.concatenate(s_parts, axis=1), cbs)
        return carry

    lax.fori_loop(0, n_chunks, chunk, 0)

    @pl.when(b == nb - 1)
    def _():
        for extra in range(RING - 1):
            wait_chunk((total + extra) % RING)

    pad = LANES - n_new
    cb = jnp.concatenate([ckvn_ref[...], jnp.zeros((pad, KV_LORA), F32)], axis=0).astype(BF16)
    s = scores(cb, kpen_ref[...], cost_ref[...], sint_ref[...])
    key = lax.broadcasted_iota(jnp.int32, (rows, LANES), 1)
    tok = lax.broadcasted_iota(jnp.int32, (rows, LANES), 0) % n_new
    update(jnp.where(key <= tok, s, NEG_INF), [cb])
    o_lat = (acc_sc[...] / l_sc[:, :1]).astype(BF16)
    full = _dot(o_lat, wv_ref[...]).reshape(N_HEADS, n_new, ATTN_WIDTH)
    hd = lax.broadcasted_iota(jnp.int32, (N_HEADS, n_new, ATTN_WIDTH), 0)
    colh = lax.broadcasted_iota(jnp.int32, (N_HEADS, n_new, ATTN_WIDTH), 2) // V_HEAD
    o_ref[...] = jnp.sum(jnp.where(hd == colh, full, 0.0), axis=0)


def _sample_attn(page_table, cache_ckv, cache_kpe_t, qlat, qf, ckv_new, kpe_new_t, tabs, wts):
    nb, n_pages = page_table.shape
    page = cache_ckv.shape[1]
    n_new = ckv_new.shape[1]
    npg = PAGES_PER_STEP
    assert page == LANES and n_pages % npg == 0 and nb * (n_pages // npg) >= RING
    cos_t, sin_t, cos_tail, sin_tail = tabs
    rows = N_HEADS * n_new
    const = lambda shape: pl.BlockSpec(shape, lambda b, pt, nd=len(shape): (0,) * nd)
    in_specs = [
        pl.BlockSpec(memory_space=pl.ANY), pl.BlockSpec(memory_space=pl.ANY),
        pl.BlockSpec((N_HEADS, n_new, KV_LORA), lambda b, pt: (0, b, 0)),
        pl.BlockSpec((N_HEADS, n_new, LANES), lambda b, pt: (0, b, 0)),
        pl.BlockSpec((None, n_new, KV_LORA), lambda b, pt: (b, 0, 0)),
        pl.BlockSpec((None, ROPE_DIM, LANES), lambda b, pt: (b, 0, 0)),
        const(cos_t.shape), const(sin_t.shape), const(cos_tail.shape), const(sin_tail.shape),
    ] + [const(w.shape) for w in wts]
    grid_spec = pltpu.PrefetchScalarGridSpec(
        num_scalar_prefetch=1, grid=(nb,), in_specs=in_specs,
        out_specs=pl.BlockSpec((None, n_new, ATTN_WIDTH), lambda b, pt: (b, 0, 0)),
        scratch_shapes=[pltpu.VMEM((RING, npg * page, KV_LORA), F32),
                        pltpu.VMEM((RING, npg, ROPE_DIM, page), F32),
                        pltpu.SemaphoreType.DMA((RING,)), pltpu.SemaphoreType.DMA((RING,)),
                        pltpu.VMEM((N_HEADS * QK_NOPE + rows, KV_LORA), BF16),
                        pltpu.VMEM((rows, LANES), F32), pltpu.VMEM((rows, LANES), F32),
                        pltpu.VMEM((rows, KV_LORA), F32)])
    return pl.pallas_call(
        functools.partial(_sample_attn_body, n_pages=n_pages, n_new=n_new),
        out_shape=jax.ShapeDtypeStruct((nb, n_new, ATTN_WIDTH), F32),
        grid_spec=grid_spec,
        compiler_params=pltpu.CompilerParams(dimension_semantics=("arbitrary",),
                                             vmem_limit_bytes=VMEM_LIMIT),
        name="sample_attn",
    )(page_table, cache_ckv, cache_kpe_t, qlat, qf, ckv_new, kpe_new_t, cos_t, sin_t, cos_tail, sin_tail, *wts)


def _post_attn_body(xmid_ref, attn_p_ref, xs_ref, attn_s_ref, pool_s_ref, wouta_ref, woutp_ref, gffn_ref,
                    wrh_ref, wrl_ref, br_ref,
                    x1_ref, h2_ref, topi_ref, gate_ref, rank_ref, cnt_ref, base_sc, xm_sc, at_sc,
                    *, tm, n_prompt_tiles):
    i = pl.program_id(0)

    @pl.when(i == 0)
    def _():
        base_sc[...] = jnp.zeros((N_EXPERTS, LANES), F32)

    @pl.when(i < n_prompt_tiles)
    def _():
        xm_sc[...] = xmid_ref[...]
        at_sc[...] = attn_p_ref[...]

    @pl.when(i >= n_prompt_tiles)
    def _():
        xm_sc[...] = xs_ref[...] + _dot(pool_s_ref[...].astype(BF16), woutp_ref[...])
        at_sc[...] = attn_s_ref[...].astype(BF16)

    x1 = xm_sc[...] + _dot(at_sc[...], wouta_ref[...])
    x1_ref[...] = x1
    h2 = _rms(x1, D_MODEL) * gffn_ref[...]
    h2_ref[...] = h2
    h_hi = h2.astype(BF16)
    h_lo = (h2 - h_hi.astype(F32)).astype(BF16)
    lg = (_dot_nt(wrh_ref[...], h_hi) + _dot_nt(wrh_ref[...], h_lo) + _dot_nt(wrl_ref[...], h_hi)
          + br_ref[:, :1])
    eid = lax.broadcasted_iota(jnp.int32, (N_EXPERTS, tm), 0)
    vals, idxs, hots = [], [], []
    for _ in range(TOP_K):
        mx = jnp.max(lg, axis=0, keepdims=True)
        ix = jnp.min(jnp.where(lg == mx, eid, N_EXPERTS), axis=0, keepdims=True)
        hot = eid == ix
        lg = jnp.where(hot, NEG_INF, lg)
        vals.append(mx)
        idxs.append(ix)
        hots.append(hot)
    ex = [jnp.exp(v - vals[0]) for v in vals]
    den = ex[0] + ex[1] + ex[2] + ex[3]
    gate_ref[...] = jnp.concatenate([e / den for e in ex], axis=0)
    topi_ref[...] = jnp.concatenate(idxs, axis=0)

    anyhot = (hots[0] | hots[1] | hots[2] | hots[3])
    hot_f = jnp.where(anyhot, 1.0, 0.0)
    upper = jnp.where(lax.broadcasted_iota(jnp.int32, (tm, tm), 0) < lax.broadcasted_iota(jnp.int32, (tm, tm), 1),
                      1.0, 0.0).astype(BF16)
    before = _dot(hot_f.astype(BF16), upper) + base_sc[:, :1]
    ranks = [jnp.sum(jnp.where(h, before, 0.0), axis=0, keepdims=True) for h in hots]
    rank_ref[...] = jnp.concatenate(ranks, axis=0).astype(jnp.int32)
    base_sc[...] = base_sc[...] + jnp.sum(hot_f, axis=1, keepdims=True)
    cnt_ref[...] = base_sc[...].astype(jnp.int32)


def _post_attn(xmid_p, attn_p, x_s, attn_s, pool_s, wts, tm=TM_TOK):
    n_p, n_s = xmid_p.shape[0], x_s.shape[0]
    npt, nst = n_p // tm, n_s // tm
    nt = n_p + n_s
    p_idx = lambda i: (jnp.minimum(i, npt - 1), 0)
    s_idx = lambda i: (jnp.maximum(i - npt, 0), 0)
    in_specs = [
        pl.BlockSpec((tm, D_MODEL), p_idx), pl.BlockSpec((tm, ATTN_WIDTH), p_idx),
        pl.BlockSpec((tm, D_MODEL), s_idx), pl.BlockSpec((tm, ATTN_WIDTH), s_idx),
        pl.BlockSpec((tm, POOL_WIDTH), s_idx),
    ] + [_full_spec(w.shape) for w in wts]
    col = lambda rows: pl.BlockSpec((rows, tm), lambda i: (0, i))
    out_shape = (
        jax.ShapeDtypeStruct((nt, D_MODEL), F32), jax.ShapeDtypeStruct((nt, D_MODEL), F32),
        jax.ShapeDtypeStruct((TOP_K, nt), jnp.int32), jax.ShapeDtypeStruct((TOP_K, nt), F32),
        jax.ShapeDtypeStruct((TOP_K, nt), jnp.int32), jax.ShapeDtypeStruct((N_EXPERTS, LANES), jnp.int32),
    )
    out_specs = (pl.BlockSpec((tm, D_MODEL), lambda i: (i, 0)), pl.BlockSpec((tm, D_MODEL), lambda i: (i, 0)),
                 col(TOP_K), col(TOP_K), col(TOP_K), _full_spec((N_EXPERTS, LANES)))
    return pl.pallas_call(
        functools.partial(_post_attn_body, tm=tm, n_prompt_tiles=npt),
        out_shape=out_shape, grid=(npt + nst,), in_specs=in_specs, out_specs=out_specs,
        scratch_shapes=[pltpu.VMEM((N_EXPERTS, LANES), F32), pltpu.VMEM((tm, D_MODEL), F32),
                        pltpu.VMEM((tm, ATTN_WIDTH), BF16)],
        compiler_params=pltpu.CompilerParams(dimension_semantics=("arbitrary",),
                                             vmem_limit_bytes=VMEM_LIMIT),
        name="post_attn",
    )(xmid_p, attn_p, x_s, attn_s, pool_s, *wts)


def _invert_body(dest_ref, zeros_hbm, rt_ref, sem, *, tm):
    i = pl.program_id(0)

    @pl.when(i == 0)
    def _():
        cp = pltpu.make_async_copy(zeros_hbm, rt_ref, sem)
        cp.start()
        cp.wait()

    base = i * tm

    def body(i, carry):
        for j in range(GATHER_UNROLL // TOP_K):
            t = i * (GATHER_UNROLL // TOP_K) + j
            for k in range(TOP_K):
                rt_ref[dest_ref[k, t]] = base + t
        return carry

    lax.fori_loop(0, tm * TOP_K // GATHER_UNROLL, body, 0)


def _invert(dest, n_rows, tm=TM_DISP):
    nt = dest.shape[1]
    return pl.pallas_call(
        functools.partial(_invert_body, tm=tm),
        out_shape=jax.ShapeDtypeStruct((n_rows,), jnp.int32),
        grid=(nt // tm,),
        in_specs=[pl.BlockSpec((TOP_K, tm), lambda i: (0, i), memory_space=pltpu.SMEM),
                  pl.BlockSpec(memory_space=pl.ANY)],
        out_specs=pl.BlockSpec(memory_space=pltpu.SMEM),
        scratch_shapes=[pltpu.SemaphoreType.DMA(())],
        compiler_params=pltpu.CompilerParams(dimension_semantics=("arbitrary",)),
        name="invert",
    )(dest, jnp.zeros((n_rows,), jnp.int32))


def _experts_body(be_ref, nu_ref, rt_cur_ref, rt_nxt_ref, h2_ref, wg_ref, bg_ref, wu_ref, bu_ref, wd_ref, bd_ref,
                  y_ref, xbuf, sem, *, tb):
    blk = pl.program_id(0)
    slot = blk % 2
    n_used = nu_ref[0]

    def gather(rt_ref, s):
        def body(i, carry):
            for j in range(GATHER_UNROLL):
                r = i * GATHER_UNROLL + j
                pltpu.make_async_copy(h2_ref.at[pl.ds(rt_ref[r], 1)], xbuf.at[s, pl.ds(r, 1)],
                                      sem.at[s]).start(priority=j % 2)
            return carry
        lax.fori_loop(0, tb // GATHER_UNROLL, body, 0)

    @pl.when((blk == 0) & (n_used > 0))
    def _():
        gather(rt_cur_ref, 0)

    @pl.when(blk + 1 < n_used)
    def _():
        gather(rt_nxt_ref, 1 - slot)

    @pl.when(blk < n_used)
    def _():
        pltpu.make_async_copy(h2_ref.at[pl.ds(0, tb)], xbuf.at[slot], sem.at[slot]).wait()
        xb = xbuf[slot].astype(BF16)
        g = jnp.minimum(_dot(xb, wg_ref[...]) + bg_ref[...], SWIGLU_LIMIT)
        u = jnp.clip(_dot(xb, wu_ref[...]) + bu_ref[...], -SWIGLU_LIMIT, SWIGLU_LIMIT)
        act = (u + 1.0) * (g * jax.nn.sigmoid(SWIGLU_ALPHA * g))
        y_ref[...] = _dot(act.astype(BF16), wd_ref[...]) + bd_ref[...]

    @pl.when(blk >= n_used)
    def _():
        y_ref[...] = jnp.zeros(y_ref.shape, F32)


def _experts(block_expert, n_used, row_token, h2, wg, bg, wu, bu, wd, bd, tb=TB):
    r = row_token.shape[0]
    nblk = r // tb
    w_spec = pl.BlockSpec((None, D_MODEL, D_MODEL), lambda b, be, nu: (be[b], 0, 0))
    b_spec = pl.BlockSpec((None, 1, D_MODEL), lambda b, be, nu: (be[b], 0, 0))
    rt_cur = pl.BlockSpec((tb,), lambda b, be, nu: (b,), memory_space=pltpu.SMEM)
    rt_nxt = pl.BlockSpec((tb,), lambda b, be, nu: (jnp.minimum(b + 1, nblk - 1),), memory_space=pltpu.SMEM)
    grid_spec = pltpu.PrefetchScalarGridSpec(
        num_scalar_prefetch=2, grid=(nblk,),
        in_specs=[rt_cur, rt_nxt, pl.BlockSpec(memory_space=pl.ANY),
                  w_spec, b_spec, w_spec, b_spec, w_spec, b_spec],
        out_specs=pl.BlockSpec((tb, D_MODEL), lambda b, be, nu: (b, 0)),
        scratch_shapes=[pltpu.VMEM((2, tb, D_MODEL), F32), pltpu.SemaphoreType.DMA((2,))])
    return pl.pallas_call(
        functools.partial(_experts_body, tb=tb),
        out_shape=jax.ShapeDtypeStruct((r, D_MODEL), F32), grid_spec=grid_spec,
        compiler_params=pltpu.CompilerParams(dimension_semantics=("arbitrary",),
                                             vmem_limit_bytes=VMEM_LIMIT),
        name="experts",
    )(block_expert, n_used, row_token, row_token, h2, wg, bg, wu, bu, wd, bd)


def _final_body(dest_ref, x1_ref, gate_ref, p_ref, ys_ref, gple_ref, wpg_ref, wpp_ref, o_ref, gbuf, sem, *, tm):
    def issue(i, carry):
        for j in range(GATHER_UNROLL // TOP_K):
            t = i * (GATHER_UNROLL // TOP_K) + j
            for k in range(TOP_K):
                pltpu.make_async_copy(ys_ref.at[pl.ds(dest_ref[k, t], 1)], gbuf.at[k, pl.ds(t, 1)],
                                      sem).start(priority=k % 2)
        return carry

    lax.fori_loop(0, tm * TOP_K // GATHER_UNROLL, issue, 0)
    pp = _dot(p_ref[...].astype(BF16), wpp_ref[...])
    for k in range(TOP_K):
        pltpu.make_async_copy(ys_ref.at[pl.ds(0, tm)], gbuf.at[k], sem).wait()
    gates = gate_ref[...]
    x2 = x1_ref[...]
    for k in range(TOP_K):
        x2 = x2 + gates[:, k:k + 1] * gbuf[k]
    hn = (_rms(x2, D_MODEL) * gple_ref[...]).astype(BF16)
    o_ref[...] = x2 + jax.nn.sigmoid(_dot(hn, wpg_ref[...])) * pp


def _final(dest, x1, gates_t, p, ys, wts, tile_off, n_tok, tm=TM_TOK):
    in_specs = [
        pl.BlockSpec((TOP_K, tm), lambda i: (0, i + tile_off), memory_space=pltpu.SMEM),
        pl.BlockSpec((tm, D_MODEL), lambda i: (i + tile_off, 0)),
        pl.BlockSpec((tm, TOP_K), lambda i: (i + tile_off, 0)),
        pl.BlockSpec((tm, PLE_DIM), lambda i: (i, 0)),
        pl.BlockSpec(memory_space=pl.ANY),
    ] + [_full_spec(w.shape) for w in wts]
    return pl.pallas_call(
        functools.partial(_final_body, tm=tm),
        out_shape=jax.ShapeDtypeStruct((n_tok, D_MODEL), F32),
        grid=(n_tok // tm,), in_specs=in_specs,
        out_specs=pl.BlockSpec((tm, D_MODEL), lambda i: (i, 0)),
        scratch_shapes=[pltpu.VMEM((TOP_K, tm, D_MODEL), F32), pltpu.SemaphoreType.DMA(())],
        compiler_params=pltpu.CompilerParams(dimension_semantics=("arbitrary",),
                                             vmem_limit_bytes=VMEM_LIMIT),
        name="final",
    )(dest, x1, gates_t, p, ys, *wts)


def _rope_angles(pos):
    inv = ROPE_THETA ** (-jnp.arange(ROPE_HALF, dtype=F32) / ROPE_HALF)
    ang = pos.astype(F32)[:, None] * inv
    return jnp.cos(ang), jnp.sin(ang)


def _rope_row_tables(pos):
    cos, sin = _rope_angles(pos)
    n = pos.shape[0]
    z = lambda w: jnp.zeros((n, w), F32)
    c_tab = jnp.concatenate([jnp.ones((n, QK_NOPE), F32), cos, cos, z(LANES - QK_HEAD)], axis=1)
    sa_tab = jnp.concatenate([z(QK_NOPE), -sin, z(ROPE_HALF), z(LANES - QK_HEAD)], axis=1)
    sb_tab = jnp.concatenate([z(QK_NOPE), z(ROPE_HALF), sin, z(LANES - QK_HEAD)], axis=1)
    return c_tab, sa_tab, sb_tab


def _pad_lanes(v, width=LANES):
    return jnp.pad(v, [(0, 0)] * (v.ndim - 1) + [(0, width - v.shape[-1])])


def kernel(x_prompt, x_sample, cache_ckv, cache_kpe, state_pool, page_table, p_prompt, p_sample, g_mix, w_in, g_q_a, w_q_b, g_kv_a, w_kv_b, g_q_head, g_k_head, w_pool, pool_scale, w_out, g_ffn, w_router, b_router, w_gate, b_gate, w_up, b_up, w_down, b_down, g_ple, w_ple_gate, w_ple_proj):
    assert x_prompt.shape[-1] == D_MODEL and g_mix.shape[0] == 1
    b, s, _ = x_prompt.shape
    nb, n_new, _ = x_sample.shape
    n_p, n_s = b * s, nb * n_new
    past = page_table.shape[1] * cache_ckv.shape[2]
    row = lambda v: v.reshape(1, -1).astype(F32)

    wi = w_in[0]
    w_cq, w_ckv = wi[:, :Q_LORA], wi[:, Q_LORA:Q_LORA + KV_LORA]
    w_kpe = wi[:, Q_LORA + KV_LORA:Q_LORA + KV_LORA + ROPE_DIM]
    w_u = wi[:, Q_LORA + KV_LORA + ROPE_DIM:]
    w_kpe128 = jnp.pad(w_kpe, ((0, 0), (QK_NOPE, LANES - QK_HEAD)))
    w_main = jnp.concatenate([w_cq, w_ckv, w_u, w_kpe128], axis=1).astype(BF16)
    wq = _pad_lanes(w_q_b[0]).reshape(Q_LORA, N_HEADS * LANES).astype(BF16)
    wk_nope = w_kv_b[0][:, :, :QK_NOPE]
    wk = _pad_lanes(wk_nope).reshape(KV_LORA, N_HEADS * LANES).astype(BF16)
    wv = w_kv_b[0][:, :, QK_NOPE:].reshape(KV_LORA, ATTN_WIDTH).astype(BF16)
    gq128 = _pad_lanes(row(g_q_head[0]))
    gk128 = _pad_lanes(row(g_k_head[0]))
    gkn128 = _pad_lanes(row(g_k_head[0][:QK_NOPE]))
    wkt = jnp.transpose(wk_nope, (1, 2, 0))
    wkt128 = jnp.pad(wkt, ((0, 0), (0, LANES - QK_NOPE), (0, 0))).astype(BF16)
    wkt_flat = wkt.reshape(N_HEADS * QK_NOPE, KV_LORA).astype(BF16)
    gpe_b = jnp.broadcast_to(g_k_head[0][QK_NOPE:].astype(F32)[:, None], (ROPE_DIM, LANES))
    wpool = w_pool[0].astype(BF16)
    pscale = row(pool_scale[0])
    wout_a = w_out[0][:ATTN_WIDTH].astype(BF16)
    wout_p = w_out[0][ATTN_WIDTH:].astype(BF16)
    wr_t = w_router[0].T.astype(F32)
    wr_hi = wr_t.astype(BF16)
    wr_lo = (wr_t - wr_hi.astype(F32)).astype(BF16)
    br_b = jnp.broadcast_to(b_router[0].astype(F32)[:, None], (N_EXPERTS, LANES))

    tabs_p = _rope_row_tables(jnp.arange(s, dtype=jnp.int32))
    pos_s = past + (jnp.arange(TM_PROJ, dtype=jnp.int32) % n_new)
    tabs_s = _rope_row_tables(pos_s)
    cos_t, sin_t = (t.T for t in _rope_angles(jnp.arange(past + LANES, dtype=jnp.int32)))
    tabs_attn = (cos_t[:, :past], sin_t[:, :past], cos_t[:, past:], sin_t[:, past:])

    ckv_p, kpe_p, q_p, k_p, v_p, xmid_p, pst_p = _proj_prompt(
        x_prompt, tabs_p,
        (row(g_mix[0]), w_main, row(g_q_a[0]), row(g_kv_a[0]), wq, gq128, wk, wv, gk128, wpool, pscale, wout_p))
    attn_p = _flash_prompt(q_p, k_p, v_p, b, s)

    xs_flat = x_sample.reshape(n_s, D_MODEL)
    ckv_s, kpe_s, u_s, qf_s, qlat_s = _proj_sample(
        xs_flat, tabs_s, (row(g_mix[0]), w_main, row(g_q_a[0]), row(g_kv_a[0]), wq, gq128, wkt128, gkn128))
    u_s3 = u_s.reshape(nb, n_new, POOL_WIDTH)
    pool_t = _pool_sample(jnp.transpose(state_pool[0], (1, 0, 2)), jnp.transpose(u_s3, (1, 0, 2)), wpool, pscale)
    pool_s = jnp.transpose(pool_t, (1, 0, 2)).reshape(n_s, POOL_WIDTH)
    kpe_new_t = _pad_lanes(jnp.transpose(kpe_s.reshape(nb, n_new, ROPE_DIM), (0, 2, 1)))
    attn_s = _sample_attn(page_table, cache_ckv[0], jnp.transpose(cache_kpe[0], (0, 2, 1)), qlat_s, qf_s,
                          ckv_s.reshape(nb, n_new, KV_LORA), kpe_new_t,
                          tabs_attn, (wkt_flat, gpe_b, wv)).reshape(n_s, ATTN_WIDTH)

    x1, h2, topi, gates, rank, counts = _post_attn(
        xmid_p, attn_p, xs_flat, attn_s, pool_s, (wout_a, wout_p, row(g_ffn[0]), wr_hi, wr_lo, br_b))
    nt = n_p + n_s
    counts = counts[:, 0]
    padded = (counts + TB - 1) // TB * TB
    ends = jnp.cumsum(padded)
    starts = ends - padded
    eids = jnp.arange(N_EXPERTS, dtype=jnp.int32)
    dest = (jnp.sum(jnp.where(topi[..., None] == eids, starts, 0), axis=-1) + rank).astype(jnp.int32)
    n_rows = nt * TOP_K + N_EXPERTS * TB
    n_blocks = n_rows // TB
    n_used = (ends[-1] // TB).astype(jnp.int32).reshape(1)
    blk_start = jnp.arange(n_blocks, dtype=jnp.int32) * TB
    blk_start = jnp.minimum(blk_start, jnp.maximum(ends[-1] - TB, 0))
    block_expert = jnp.minimum(jnp.sum((ends[None, :] <= blk_start[:, None]).astype(jnp.int32), axis=1),
                               N_EXPERTS - 1).astype(jnp.int32)

    row_token = _invert(dest, n_rows)
    b3 = lambda v: v[0].reshape(N_EXPERTS, 1, D_MODEL).astype(F32)
    ys = _experts(block_expert, n_used, row_token, h2, w_gate[0].astype(BF16), b3(b_gate), w_up[0].astype(BF16),
                  b3(b_up), w_down[0].astype(BF16), b3(b_down))
    gates_t = gates.T
    fin_w = (row(g_ple[0]), w_ple_gate[0].astype(BF16), w_ple_proj[0].astype(BF16))
    y_p = _final(dest, x1, gates_t, p_prompt[0].reshape(n_p, PLE_DIM), ys, fin_w, 0, n_p)
    y_s = _final(dest, x1, gates_t, p_sample[0].reshape(n_s, PLE_DIM), ys, fin_w, n_p // TM_TOK, n_s)

    pool_state_s = jnp.concatenate([state_pool[0][:, n_new:], u_s3], axis=1)
    return (y_p.reshape(b, s, D_MODEL), y_s.reshape(nb, n_new, D_MODEL),
            ckv_p.reshape(1, b, s, KV_LORA), kpe_p.reshape(1, b, s, ROPE_DIM), pst_p[:, HALO - POOL_PREV:][None],
            ckv_s.reshape(1, nb, n_new, KV_LORA), kpe_s.reshape(1, nb, n_new, ROPE_DIM), pool_state_s[None])
```

```python
import functools

import jax
import jax.numpy as jnp
from jax import lax
from jax.experimental import pallas as pl
from jax.experimental.pallas import tpu as pltpu

F32 = jnp.float32
BF16 = jnp.bfloat16

D_MODEL = 1024
N_HEADS = 8
QK_NOPE = 64
ROPE_DIM = 32
ROPE_HALF = ROPE_DIM // 2
QK_HEAD = QK_NOPE + ROPE_DIM
V_HEAD = 64
Q_LORA = 384
KV_LORA = 256
ATTN_WIDTH = N_HEADS * V_HEAD
ROPE_THETA = 10000.0
SM_SCALE = QK_HEAD ** -0.5
POOL_WINDOWS = (2, 4, 8, 16)
POOL_GROUP_DIM = 128
POOL_WIDTH = 512
POOL_PREV = 15
N_EXPERTS = 32
TOP_K = 4
SWIGLU_LIMIT = 7.0
SWIGLU_ALPHA = 1.702
PLE_DIM = 256
EPS = 1e-6

LANES = 128
HALO = 16
MAIN_COLS = Q_LORA + KV_LORA + POOL_WIDTH + LANES
VMEM_LIMIT = 56 * 1024 * 1024

TM_PROJ = 256
TQ, TK = 256, 512
PAGES_PER_STEP = 16
RING = 3
SUB = 256
TM_TOK = 256
TM_DISP = 512
TB = 256
GATHER_UNROLL = 8

NEG_INF = float("-inf")


def _dot(a, b):
    return jnp.dot(a, b, preferred_element_type=F32)


def _dot_nt(a, b):
    return lax.dot_general(a, b, (((1,), (1,)), ((), ())), preferred_element_type=F32)


def _rms(x, n):
    return x * lax.rsqrt(jnp.sum(x * x, axis=-1, keepdims=True) * (1.0 / n) + EPS)


def _rope_rows(x, c_tab, sa_tab, sb_tab):
    return (x * c_tab + pltpu.roll(x, LANES - ROPE_HALF, 1) * sa_tab
            + pltpu.roll(x, ROPE_HALF, 1) * sb_tab)


def _proj_body(*refs, tm, seq_mode):
    if seq_mode:
        (x_ref, c_ref, sa_ref, sb_ref, gmix_ref, wmain_ref, gqa_ref, gkva_ref, wq_ref, gq_ref,
         wk_ref, wv_ref, gk_ref, wpool_ref, pscale_ref, woutp_ref,
         ckv_ref, kpe_ref, q_ref, k_ref, v_ref, xmid_ref, pst_ref, ubuf) = refs
    else:
        (x_ref, c_ref, sa_ref, sb_ref, gmix_ref, wmain_ref, gqa_ref, gkva_ref, wq_ref, gq_ref,
         wkt_ref, gkn_ref,
         ckv_ref, kpe_ref, u_ref, q_ref, qlat_ref) = refs

    x = x_ref[...]
    h = (_rms(x, D_MODEL) * gmix_ref[...]).astype(BF16)
    z = _dot(h, wmain_ref[...])
    cqn = (_rms(z[:, :Q_LORA], Q_LORA) * gqa_ref[...]).astype(BF16)
    ckvn = _rms(z[:, Q_LORA:Q_LORA + KV_LORA], KV_LORA) * gkva_ref[...]
    u = z[:, Q_LORA + KV_LORA:Q_LORA + KV_LORA + POOL_WIDTH]
    kpe128 = z[:, Q_LORA + KV_LORA + POOL_WIDTH:]
    ckv_ref[...] = ckvn
    kpe_ref[...] = pltpu.roll(kpe128, LANES - QK_NOPE, 1)[:, :ROPE_DIM]

    c_tab, sa_tab, sb_tab = c_ref[...], sa_ref[...], sb_ref[...]
    q_all = _dot(cqn, wq_ref[...])
    gq = gq_ref[...]
    for hd in range(N_HEADS):
        qh = q_all[:, hd * LANES:(hd + 1) * LANES]
        qh = _rope_rows(_rms(qh, QK_HEAD) * gq, c_tab, sa_tab, sb_tab) * SM_SCALE
        if seq_mode:
            q_ref[hd] = qh.astype(BF16)
        else:
            q_ref[hd] = qh
            qlat_ref[hd] = _dot((qh * gkn_ref[...]).astype(BF16), wkt_ref[hd])

    if not seq_mode:
        u_ref[...] = u
        return

    ckvb = ckvn.astype(BF16)
    k_all = _dot(ckvb, wk_ref[...])
    gk = gk_ref[...]
    for hd in range(N_HEADS):
        kh = k_all[:, hd * LANES:(hd + 1) * LANES] + kpe128
        kh = _rope_rows(_rms(kh, QK_HEAD) * gk, c_tab, sa_tab, sb_tab)
        k_ref[hd] = kh.astype(BF16)
    v_all = _dot(ckvb, wv_ref[...])
    for hp in range(N_HEADS // 2):
        v_ref[hp] = v_all[:, hp * LANES:(hp + 1) * LANES].astype(BF16)

    j = pl.program_id(1)

    @pl.when(j == 0)
    def _():
        ubuf[0:HALO, :] = jnp.zeros((HALO, POOL_WIDTH), F32)

    ubuf[HALO:HALO + tm, :] = u
    pos1 = (j * tm + 1 + lax.broadcasted_iota(jnp.int32, (tm, 1), 0)).astype(F32)
    pool_parts = []
    for g, w in enumerate(POOL_WINDOWS):
        sl = slice(g * POOL_GROUP_DIM, (g + 1) * POOL_GROUP_DIM)
        ug = u[:, sl]
        acc = ug
        for back in range(1, w):
            acc = acc + ubuf[HALO - back:HALO - back + tm, sl]
        d = acc / jnp.minimum(float(w), pos1) - ug
        pool_parts.append(_dot(d.astype(BF16), wpool_ref[g]) * pscale_ref[:, sl])
    pool_out = jnp.concatenate(pool_parts, axis=1).astype(BF16)
    xmid_ref[...] = x + _dot(pool_out, woutp_ref[...])

    tail = ubuf[tm:tm + HALO, :]
    ubuf[0:HALO, :] = tail

    @pl.when(j == pl.num_programs(1) - 1)
    def _():
        pst_ref[...] = tail


def _full_spec(shape):
    nd = len(shape)
    return pl.BlockSpec(shape, lambda *_: (0,) * nd)


def _proj_prompt(x, tabs, wts, tm=TM_PROJ):
    b, s, _ = x.shape
    n = b * s
    nj = s // tm
    c_tab, sa_tab, sb_tab = tabs
    row = lambda width: pl.BlockSpec((tm, width), lambda bi, j: (bi * nj + j, 0))
    tab = pl.BlockSpec((tm, LANES), lambda bi, j: (j, 0))
    heads = lambda nh, width: pl.BlockSpec((nh, tm, width), lambda bi, j: (0, bi * nj + j, 0))
    in_specs = [pl.BlockSpec((None, tm, D_MODEL), lambda bi, j: (bi, j, 0)), tab, tab, tab] + [
        _full_spec(w.shape) for w in wts]
    out_shape = (
        jax.ShapeDtypeStruct((n, KV_LORA), F32),
        jax.ShapeDtypeStruct((n, ROPE_DIM), F32),
        jax.ShapeDtypeStruct((N_HEADS, n, LANES), BF16),
        jax.ShapeDtypeStruct((N_HEADS, n, LANES), BF16),
        jax.ShapeDtypeStruct((N_HEADS // 2, n, LANES), BF16),
        jax.ShapeDtypeStruct((n, D_MODEL), F32),
        jax.ShapeDtypeStruct((b, HALO, POOL_WIDTH), F32),
    )
    out_specs = (row(KV_LORA), row(ROPE_DIM), heads(N_HEADS, LANES), heads(N_HEADS, LANES),
                 heads(N_HEADS // 2, LANES), row(D_MODEL),
                 pl.BlockSpec((None, HALO, POOL_WIDTH), lambda bi, j: (bi, 0, 0)))
    return pl.pallas_call(
        functools.partial(_proj_body, tm=tm, seq_mode=True),
        out_shape=out_shape, grid=(b, nj), in_specs=in_specs, out_specs=out_specs,
        scratch_shapes=[pltpu.VMEM((tm + HALO, POOL_WIDTH), F32)],
        compiler_params=pltpu.CompilerParams(dimension_semantics=("arbitrary", "arbitrary"),
                                             vmem_limit_bytes=VMEM_LIMIT),
        name="proj_prompt",
    )(x, c_tab, sa_tab, sb_tab, *wts)


def _proj_sample(x, tabs, wts, tm=TM_PROJ):
    n = x.shape[0]
    c_tab, sa_tab, sb_tab = tabs
    row = lambda width: pl.BlockSpec((tm, width), lambda i: (i, 0))
    heads = lambda width: pl.BlockSpec((N_HEADS, tm, width), lambda i: (0, i, 0))
    in_specs = [row(D_MODEL), _full_spec((tm, LANES)), _full_spec((tm, LANES)), _full_spec((tm, LANES))] + [
        _full_spec(w.shape) for w in wts]
    out_shape = (
        jax.ShapeDtypeStruct((n, KV_LORA), F32),
        jax.ShapeDtypeStruct((n, ROPE_DIM), F32),
        jax.ShapeDtypeStruct((n, POOL_WIDTH), F32),
        jax.ShapeDtypeStruct((N_HEADS, n, LANES), F32),
        jax.ShapeDtypeStruct((N_HEADS, n, KV_LORA), F32),
    )
    out_specs = (row(KV_LORA), row(ROPE_DIM), row(POOL_WIDTH), heads(LANES), heads(KV_LORA))
    return pl.pallas_call(
        functools.partial(_proj_body, tm=tm, seq_mode=False),
        out_shape=out_shape, grid=(n // tm,), in_specs=in_specs, out_specs=out_specs,
        compiler_params=pltpu.CompilerParams(dimension_semantics=("arbitrary",),
                                             vmem_limit_bytes=VMEM_LIMIT),
        name="proj_sample",
    )(x, c_tab, sa_tab, sb_tab, *wts)


def _flash_body(q_ref, k_ref, v_ref, o_ref, m_sc, l_sc, acc_sc, *, tq, tk):
    qi = pl.program_id(2)
    n_full = (qi * tq) // tk
    row = qi * tq + lax.broadcasted_iota(jnp.int32, (tq, tk), 0)
    col = lax.broadcasted_iota(jnp.int32, (tq, tk), 1)
    m_sc[...] = jnp.full(m_sc.shape, NEG_INF, F32)
    l_sc[...] = jnp.zeros(l_sc.shape, F32)
    acc_sc[...] = jnp.zeros(acc_sc.shape, F32)

    def step(ki, masked):
        start = pl.multiple_of(ki * tk, tk)
        v = v_ref[pl.ds(start, tk), :]
        for hh in range(2):
            s = _dot_nt(q_ref[hh], k_ref[hh, pl.ds(start, tk), :])
            if masked:
                s = jnp.where(col + start <= row, s, NEG_INF)
            m_prev = m_sc[hh]
            m_new = jnp.maximum(m_prev, jnp.max(s, axis=-1, keepdims=True))
            alpha = jnp.exp(m_prev - m_new)
            p = jnp.exp(s - m_new)
            l_sc[hh] = alpha * l_sc[hh] + jnp.sum(p, axis=-1, keepdims=True)
            acc_sc[hh] = alpha * acc_sc[hh] + _dot(p.astype(BF16), v)
            m_sc[hh] = m_new

    def loop_body(ki, carry):
        step(ki, False)
        return carry

    lax.fori_loop(0, n_full, loop_body, 0)
    step(n_full, True)
    lane = lax.broadcasted_iota(jnp.int32, (tq, LANES), 1)
    o_ref[...] = jnp.where(lane < V_HEAD, acc_sc[0] / l_sc[0], acc_sc[1] / l_sc[1]).astype(o_ref.dtype)


def _flash_prompt(q, k, v, b, s, tq=TQ, tk=TK):
    n = b * s
    nq = s // tq
    return pl.pallas_call(
        functools.partial(_flash_body, tq=tq, tk=tk),
        out_shape=jax.ShapeDtypeStruct((n, ATTN_WIDTH), BF16),
        grid=(b, N_HEADS // 2, nq),
        in_specs=[
            pl.BlockSpec((2, tq, LANES), lambda bi, hp, qi: (hp, bi * nq + qi, 0)),
            pl.BlockSpec((2, s, LANES), lambda bi, hp, qi: (hp, bi, 0)),
            pl.BlockSpec((None, s, LANES), lambda bi, hp, qi: (hp, bi, 0)),
        ],
        out_specs=pl.BlockSpec((tq, LANES), lambda bi, hp, qi: (bi * nq + qi, hp)),
        scratch_shapes=[pltpu.VMEM((2, tq, 1), F32), pltpu.VMEM((2, tq, 1), F32), pltpu.VMEM((2, tq, LANES), F32)],
        compiler_params=pltpu.CompilerParams(dimension_semantics=("arbitrary",) * 3,
                                             vmem_limit_bytes=VMEM_LIMIT),
        name="flash_prompt",
    )(q, k, v)


def _pool_sample_body(st_ref, u_ref, wpool_ref, pscale_ref, o_ref, *, n_new):
    for t in range(n_new):
        for g, w in enumerate(POOL_WINDOWS):
            sl = slice(g * POOL_GROUP_DIM, (g + 1) * POOL_GROUP_DIM)
            ug = u_ref[t, :, sl]
            acc = ug
            for back in range(1, w):
                src = t - back
                acc = acc + (u_ref[src, :, sl] if src >= 0 else st_ref[POOL_PREV + src, :, sl])
            d = acc / float(w) - ug
            o_ref[t, :, sl] = _dot(d.astype(BF16), wpool_ref[g]) * pscale_ref[:, sl]


def _pool_sample(state_t, u_t, wpool, pscale):
    n_new, nb, _ = u_t.shape
    return pl.pallas_call(
        functools.partial(_pool_sample_body, n_new=n_new),
        out_shape=jax.ShapeDtypeStruct((n_new, nb, POOL_WIDTH), F32),
        grid=(1,),
        in_specs=[_full_spec(state_t.shape), _full_spec(u_t.shape), _full_spec(wpool.shape),
                  _full_spec(pscale.shape)],
        out_specs=_full_spec((n_new, nb, POOL_WIDTH)),
        compiler_params=pltpu.CompilerParams(dimension_semantics=("arbitrary",),
                                             vmem_limit_bytes=VMEM_LIMIT),
        name="pool_sample",
    )(state_t, u_t, wpool, pscale)


def _sample_attn_body(pt_ref, ckv_hbm, kpe_hbm, qlat_ref, q_ref, ckvn_ref, kpen_ref, cos_ref, sin_ref,
                      cost_ref, sint_ref, wkt_ref, gpe_ref, wv_ref, o_ref,
                      cbuf, kbuf, sem_c, sem_k, a_sc, m_sc, l_sc, acc_sc, cbb0, cbb1, sb0, sb1,
                      *, n_pages, n_new):
    npg = PAGES_PER_STEP
    page = LANES
    n_chunks = n_pages // npg
    b = pl.program_id(0)
    nb = pl.num_programs(0)
    rows = N_HEADS * n_new

    def page_copies(seq, chunk, slot):
        cps = []
        for i in range(npg):
            pg = pt_ref[seq, chunk * npg + i]
            cps.append(pltpu.make_async_copy(ckv_hbm.at[pg], cbuf.at[slot, pl.ds(i * page, page)], sem_c.at[slot]))
            cps.append(pltpu.make_async_copy(kpe_hbm.at[pg], kbuf.at[slot, i], sem_k.at[slot]))
        return cps

    def start_chunk(seq, chunk, slot):
        for cp in page_copies(seq, chunk, slot):
            cp.start()

    def wait_chunk(slot):
        for cp in page_copies(0, 0, slot):
            cp.wait()

    total = nb * n_chunks

    @pl.when(b == 0)
    def _():
        for g0 in range(RING - 1):
            start_chunk(g0 // n_chunks, g0 % n_chunks, g0 % RING)

    a_sc[0:N_HEADS * QK_NOPE, :] = wkt_ref[...]
    a_sc[N_HEADS * QK_NOPE:, :] = qlat_ref[...].reshape(rows, KV_LORA).astype(BF16)
    m_sc[...] = jnp.full((rows, LANES), NEG_INF, F32)
    l_sc[...] = jnp.zeros((rows, LANES), F32)
    acc_sc[...] = jnp.zeros((rows, KV_LORA), F32)
    qpe = q_ref[...].reshape(rows, LANES)[:, QK_NOPE:QK_HEAD].astype(BF16)
    g1 = gpe_ref[0:ROPE_HALF, :]
    g2 = gpe_ref[ROPE_HALF:ROPE_DIM, :]

    def scores(cb, kpt, cos, sin):
        ck = cb.shape[0]
        r = _dot_nt(a_sc[...], cb)
        kt = r[:N_HEADS * QK_NOPE]
        ssn = jnp.sum((kt * kt).reshape(N_HEADS, QK_NOPE, ck), axis=1)
        x1 = kpt[0:ROPE_HALF]
        x2 = kpt[ROPE_HALF:ROPE_DIM]
        ssp = jnp.sum(x1 * x1 + x2 * x2, axis=0, keepdims=True)
        rs = lax.rsqrt((ssn + ssp) * (1.0 / QK_HEAD) + EPS)
        reps = ck // LANES
        x1 = x1 * jnp.tile(g1, (1, reps))
        x2 = x2 * jnp.tile(g2, (1, reps))
        rot = jnp.concatenate([x1 * cos - x2 * sin, x2 * cos + x1 * sin], axis=0).astype(BF16)
        s = r[N_HEADS * QK_NOPE:] + _dot(qpe, rot)
        s = s.reshape(N_HEADS, n_new, ck) * rs[:, None, :]
        return s.reshape(rows, ck)

    def update(s_all, cbs):
        m_prev = m_sc[:, :1]
        m_new = jnp.maximum(m_prev, jnp.max(s_all, axis=-1, keepdims=True))
        alpha = jnp.exp(m_prev - m_new)
        p = jnp.exp(s_all - m_new)
        l_sc[...] = jnp.broadcast_to(alpha * l_sc[:, :1] + jnp.sum(p, axis=-1, keepdims=True),
                                     (rows, LANES))
        acc = alpha * acc_sc[...]
        off = 0
        for cb in cbs:
            ck = cb.shape[0]
            acc = acc + _dot(p[:, off:off + ck].astype(BF16), cb)
            off += ck
        acc_sc[...] = acc
        m_sc[...] = jnp.broadcast_to(m_new, (rows, LANES))

    pages_per_sub = SUB // page

    n_sub = npg // pages_per_sub

    def scores_chunk(jc, cbb, s_buf):
        g = b * n_chunks + jc
        slot = g % RING
        wait_chunk(slot)
        nxt = jnp.minimum(g + RING - 1, total - 1)
        start_chunk(nxt // n_chunks, nxt % n_chunks, (g + RING - 1) % RING)
        for sc in range(n_sub):
            cb = cbuf[slot, sc * SUB:(sc + 1) * SUB, :].astype(BF16)
            cbb[sc * SUB:(sc + 1) * SUB, :] = cb
            kpt = jnp.concatenate([kbuf[slot, sc * pages_per_sub + i] for i in range(pages_per_sub)], axis=1)
            start = pl.multiple_of(jc * (npg * page) + sc * SUB, SUB)
            s_buf[:, sc * SUB:(sc + 1) * SUB] = scores(
                cb, kpt, cos_ref[:, pl.ds(start, SUB)], sin_ref[:, pl.ds(start, SUB)])

    def update_chunk(cbb, s_buf):
        update(s_buf[...], [cbb[sc * SUB:(sc + 1) * SUB, :] for sc in range(n_sub)])

    scores_chunk(0, cbb0, sb0)

    def pair(i, carry):
        scores_chunk(2 * i + 1, cbb1, sb1)
        update_chunk(cbb0, sb0)
        scores_chunk(2 * i + 2, cbb0, sb0)
        update_chunk(cbb1, sb1)
        return carry

    lax.fori_loop(0, n_chunks // 2 - 1, pair, 0)
    scores_chunk(n_chunks - 1, cbb1, sb1)
    update_chunk(cbb0, sb0)
    update_chunk(cbb1, sb1)

    @pl.when(b == nb - 1)
    def _():
        for extra in range(RING - 1):
            wait_chunk((total + extra) % RING)

    pad = LANES - n_new
    cb = jnp.concatenate([ckvn_ref[...], jnp.zeros((pad, KV_LORA), F32)], axis=0).astype(BF16)
    s = scores(cb, kpen_ref[...], cost_ref[...], sint_ref[...])
    key = lax.broadcasted_iota(jnp.int32, (rows, LANES), 1)
    tok = lax.broadcasted_iota(jnp.int32, (rows, LANES), 0) % n_new
    update(jnp.where(key <= tok, s, NEG_INF), [cb])
    o_lat = (acc_sc[...] / l_sc[:, :1]).astype(BF16)
    full = _dot(o_lat, wv_ref[...]).reshape(N_HEADS, n_new, ATTN_WIDTH)
    hd = lax.broadcasted_iota(jnp.int32, (N_HEADS, n_new, ATTN_WIDTH), 0)
    colh = lax.broadcasted_iota(jnp.int32, (N_HEADS, n_new, ATTN_WIDTH), 2) // V_HEAD
    o_ref[...] = jnp.sum(jnp.where(hd == colh, full, 0.0), axis=0)


def _sample_attn(page_table, cache_ckv, cache_kpe_t, qlat, qf, ckv_new, kpe_new_t, tabs, wts):
    nb, n_pages = page_table.shape
    page = cache_ckv.shape[1]
    n_new = ckv_new.shape[1]
    npg = PAGES_PER_STEP
    assert page == LANES and n_pages % (2 * npg) == 0 and nb * (n_pages // npg) >= RING
    cos_t, sin_t, cos_tail, sin_tail = tabs
    rows = N_HEADS * n_new
    const = lambda shape: pl.BlockSpec(shape, lambda b, pt, nd=len(shape): (0,) * nd)
    in_specs = [
        pl.BlockSpec(memory_space=pl.ANY), pl.BlockSpec(memory_space=pl.ANY),
        pl.BlockSpec((N_HEADS, n_new, KV_LORA), lambda b, pt: (0, b, 0)),
        pl.BlockSpec((N_HEADS, n_new, LANES), lambda b, pt: (0, b, 0)),
        pl.BlockSpec((None, n_new, KV_LORA), lambda b, pt: (b, 0, 0)),
        pl.BlockSpec((None, ROPE_DIM, LANES), lambda b, pt: (b, 0, 0)),
        const(cos_t.shape), const(sin_t.shape), const(cos_tail.shape), const(sin_tail.shape),
    ] + [const(w.shape) for w in wts]
    grid_spec = pltpu.PrefetchScalarGridSpec(
        num_scalar_prefetch=1, grid=(nb,), in_specs=in_specs,
        out_specs=pl.BlockSpec((None, n_new, ATTN_WIDTH), lambda b, pt: (b, 0, 0)),
        scratch_shapes=[pltpu.VMEM((RING, npg * page, KV_LORA), F32),
                        pltpu.VMEM((RING, npg, ROPE_DIM, page), F32),
                        pltpu.SemaphoreType.DMA((RING,)), pltpu.SemaphoreType.DMA((RING,)),
                        pltpu.VMEM((N_HEADS * QK_NOPE + rows, KV_LORA), BF16),
                        pltpu.VMEM((rows, LANES), F32), pltpu.VMEM((rows, LANES), F32),
                        pltpu.VMEM((rows, KV_LORA), F32),
                        pltpu.VMEM((npg * page, KV_LORA), BF16), pltpu.VMEM((npg * page, KV_LORA), BF16),
                        pltpu.VMEM((rows, npg * page), F32), pltpu.VMEM((rows, npg * page), F32)])
    return pl.pallas_call(
        functools.partial(_sample_attn_body, n_pages=n_pages, n_new=n_new),
        out_shape=jax.ShapeDtypeStruct((nb, n_new, ATTN_WIDTH), F32),
        grid_spec=grid_spec,
        compiler_params=pltpu.CompilerParams(dimension_semantics=("arbitrary",),
                                             vmem_limit_bytes=VMEM_LIMIT),
        name="sample_attn",
    )(page_table, cache_ckv, cache_kpe_t, qlat, qf, ckv_new, kpe_new_t, cos_t, sin_t, cos_tail, sin_tail, *wts)


def _post_attn_body(xmid_ref, attn_p_ref, xs_ref, attn_s_ref, pool_s_ref, wouta_ref, woutp_ref, gffn_ref,
                    wrh_ref, wrl_ref, br_ref,
                    x1_ref, h2_ref, topi_ref, gate_ref, rank_ref, cnt_ref, base_sc, xm_sc, at_sc,
                    *, tm, n_prompt_tiles):
    i = pl.program_id(0)

    @pl.when(i == 0)
    def _():
        base_sc[...] = jnp.zeros((N_EXPERTS, LANES), F32)

    @pl.when(i < n_prompt_tiles)
    def _():
        xm_sc[...] = xmid_ref[...]
        at_sc[...] = attn_p_ref[...]

    @pl.when(i >= n_prompt_tiles)
    def _():
        xm_sc[...] = xs_ref[...] + _dot(pool_s_ref[...].astype(BF16), woutp_ref[...])
        at_sc[...] = attn_s_ref[...].astype(BF16)

    x1 = xm_sc[...] + _dot(at_sc[...], wouta_ref[...])
    x1_ref[...] = x1
    h2 = _rms(x1, D_MODEL) * gffn_ref[...]
    h2_ref[...] = h2
    h_hi = h2.astype(BF16)
    h_lo = (h2 - h_hi.astype(F32)).astype(BF16)
    lg = (_dot_nt(wrh_ref[...], h_hi) + _dot_nt(wrh_ref[...], h_lo) + _dot_nt(wrl_ref[...], h_hi)
          + br_ref[:, :1])
    eid = lax.broadcasted_iota(jnp.int32, (N_EXPERTS, tm), 0)
    vals, idxs, hots = [], [], []
    for _ in range(TOP_K):
        mx = jnp.max(lg, axis=0, keepdims=True)
        ix = jnp.min(jnp.where(lg == mx, eid, N_EXPERTS), axis=0, keepdims=True)
        hot = eid == ix
        lg = jnp.where(hot, NEG_INF, lg)
        vals.append(mx)
        idxs.append(ix)
        hots.append(hot)
    ex = [jnp.exp(v - vals[0]) for v in vals]
    den = ex[0] + ex[1] + ex[2] + ex[3]
    gate_ref[...] = jnp.concatenate([e / den for e in ex], axis=0)
    topi_ref[...] = jnp.concatenate(idxs, axis=0)

    anyhot = (hots[0] | hots[1] | hots[2] | hots[3])
    hot_f = jnp.where(anyhot, 1.0, 0.0)
    upper = jnp.where(lax.broadcasted_iota(jnp.int32, (tm, tm), 0) < lax.broadcasted_iota(jnp.int32, (tm, tm), 1),
                      1.0, 0.0).astype(BF16)
    before = _dot(hot_f.astype(BF16), upper) + base_sc[:, :1]
    ranks = [jnp.sum(jnp.where(h, before, 0.0), axis=0, keepdims=True) for h in hots]
    rank_ref[...] = jnp.concatenate(ranks, axis=0).astype(jnp.int32)
    base_sc[...] = base_sc[...] + jnp.sum(hot_f, axis=1, keepdims=True)
    cnt_ref[...] = base_sc[...].astype(jnp.int32)


def _post_attn(xmid_p, attn_p, x_s, attn_s, pool_s, wts, tm=TM_TOK):
    n_p, n_s = xmid_p.shape[0], x_s.shape[0]
    npt, nst = n_p // tm, n_s // tm
    nt = n_p + n_s
    p_idx = lambda i: (jnp.minimum(i, npt - 1), 0)
    s_idx = lambda i: (jnp.maximum(i - npt, 0), 0)
    in_specs = [
        pl.BlockSpec((tm, D_MODEL), p_idx), pl.BlockSpec((tm, ATTN_WIDTH), p_idx),
        pl.BlockSpec((tm, D_MODEL), s_idx), pl.BlockSpec((tm, ATTN_WIDTH), s_idx),
        pl.BlockSpec((tm, POOL_WIDTH), s_idx),
    ] + [_full_spec(w.shape) for w in wts]
    col = lambda rows: pl.BlockSpec((rows, tm), lambda i: (0, i))
    out_shape = (
        jax.ShapeDtypeStruct((nt, D_MODEL), F32), jax.ShapeDtypeStruct((nt, D_MODEL), F32),
        jax.ShapeDtypeStruct((TOP_K, nt), jnp.int32), jax.ShapeDtypeStruct((TOP_K, nt), F32),
        jax.ShapeDtypeStruct((TOP_K, nt), jnp.int32), jax.ShapeDtypeStruct((N_EXPERTS, LANES), jnp.int32),
    )
    out_specs = (pl.BlockSpec((tm, D_MODEL), lambda i: (i, 0)), pl.BlockSpec((tm, D_MODEL), lambda i: (i, 0)),
                 col(TOP_K), col(TOP_K), col(TOP_K), _full_spec((N_EXPERTS, LANES)))
    return pl.pallas_call(
        functools.partial(_post_attn_body, tm=tm, n_prompt_tiles=npt),
        out_shape=out_shape, grid=(npt + nst,), in_specs=in_specs, out_specs=out_specs,
        scratch_shapes=[pltpu.VMEM((N_EXPERTS, LANES), F32), pltpu.VMEM((tm, D_MODEL), F32),
                        pltpu.VMEM((tm, ATTN_WIDTH), BF16)],
        compiler_params=pltpu.CompilerParams(dimension_semantics=("arbitrary",),
                                             vmem_limit_bytes=VMEM_LIMIT),
        name="post_attn",
    )(xmid_p, attn_p, x_s, attn_s, pool_s, *wts)


def _invert_body(dest_ref, zeros_hbm, rt_ref, sem, *, tm):
    i = pl.program_id(0)

    @pl.when(i == 0)
    def _():
        cp = pltpu.make_async_copy(zeros_hbm, rt_ref, sem)
        cp.start()
        cp.wait()

    base = i * tm

    def body(i, carry):
        for j in range(GATHER_UNROLL // TOP_K):
            t = i * (GATHER_UNROLL // TOP_K) + j
            for k in range(TOP_K):
                rt_ref[dest_ref[k, t]] = base + t
        return carry

    lax.fori_loop(0, tm * TOP_K // GATHER_UNROLL, body, 0)


def _invert(dest, n_rows, tm=TM_DISP):
    nt = dest.shape[1]
    return pl.pallas_call(
        functools.partial(_invert_body, tm=tm),
        out_shape=jax.ShapeDtypeStruct((n_rows,), jnp.int32),
        grid=(nt // tm,),
        in_specs=[pl.BlockSpec((TOP_K, tm), lambda i: (0, i), memory_space=pltpu.SMEM),
                  pl.BlockSpec(memory_space=pl.ANY)],
        out_specs=pl.BlockSpec(memory_space=pltpu.SMEM),
        scratch_shapes=[pltpu.SemaphoreType.DMA(())],
        compiler_params=pltpu.CompilerParams(dimension_semantics=("arbitrary",)),
        name="invert",
    )(dest, jnp.zeros((n_rows,), jnp.int32))


def _experts_body(be_ref, nu_ref, rt_cur_ref, rt_nxt_ref, h2_ref, wg_ref, bg_ref, wu_ref, bu_ref, wd_ref, bd_ref,
                  y_ref, xbuf, sem, *, tb):
    blk = pl.program_id(0)
    slot = blk % 2
    n_used = nu_ref[0]

    def gather(rt_ref, s):
        def body(i, carry):
            for j in range(GATHER_UNROLL):
                r = i * GATHER_UNROLL + j
                pltpu.make_async_copy(h2_ref.at[pl.ds(rt_ref[r], 1)], xbuf.at[s, pl.ds(r, 1)],
                                      sem.at[s]).start(priority=j % 2)
            return carry
        lax.fori_loop(0, tb // GATHER_UNROLL, body, 0)

    @pl.when((blk == 0) & (n_used > 0))
    def _():
        gather(rt_cur_ref, 0)

    @pl.when(blk + 1 < n_used)
    def _():
        gather(rt_nxt_ref, 1 - slot)

    @pl.when(blk < n_used)
    def _():
        pltpu.make_async_copy(h2_ref.at[pl.ds(0, tb)], xbuf.at[slot], sem.at[slot]).wait()
        xb = xbuf[slot].astype(BF16)
        g = jnp.minimum(_dot(xb, wg_ref[...]) + bg_ref[...], SWIGLU_LIMIT)
        u = jnp.clip(_dot(xb, wu_ref[...]) + bu_ref[...], -SWIGLU_LIMIT, SWIGLU_LIMIT)
        act = (u + 1.0) * (g * jax.nn.sigmoid(SWIGLU_ALPHA * g))
        y_ref[...] = _dot(act.astype(BF16), wd_ref[...]) + bd_ref[...]

    @pl.when(blk >= n_used)
    def _():
        y_ref[...] = jnp.zeros(y_ref.shape, F32)


def _experts(block_expert, n_used, row_token, h2, wg, bg, wu, bu, wd, bd, tb=TB):
    r = row_token.shape[0]
    nblk = r // tb
    w_spec = pl.BlockSpec((None, D_MODEL, D_MODEL), lambda b, be, nu: (be[b], 0, 0))
    b_spec = pl.BlockSpec((None, 1, D_MODEL), lambda b, be, nu: (be[b], 0, 0))
    rt_cur = pl.BlockSpec((tb,), lambda b, be, nu: (b,), memory_space=pltpu.SMEM)
    rt_nxt = pl.BlockSpec((tb,), lambda b, be, nu: (jnp.minimum(b + 1, nblk - 1),), memory_space=pltpu.SMEM)
    grid_spec = pltpu.PrefetchScalarGridSpec(
        num_scalar_prefetch=2, grid=(nblk,),
        in_specs=[rt_cur, rt_nxt, pl.BlockSpec(memory_space=pl.ANY),
                  w_spec, b_spec, w_spec, b_spec, w_spec, b_spec],
        out_specs=pl.BlockSpec((tb, D_MODEL), lambda b, be, nu: (b, 0)),
        scratch_shapes=[pltpu.VMEM((2, tb, D_MODEL), F32), pltpu.SemaphoreType.DMA((2,))])
    return pl.pallas_call(
        functools.partial(_experts_body, tb=tb),
        out_shape=jax.ShapeDtypeStruct((r, D_MODEL), F32), grid_spec=grid_spec,
        compiler_params=pltpu.CompilerParams(dimension_semantics=("arbitrary",),
                                             vmem_limit_bytes=VMEM_LIMIT),
        name="experts",
    )(block_expert, n_used, row_token, row_token, h2, wg, bg, wu, bu, wd, bd)


def _final_body(dest_ref, x1_ref, gate_ref, p_ref, ys_ref, gple_ref, wpg_ref, wpp_ref, o_ref, gbuf, sem, *, tm):
    def issue(i, carry):
        for j in range(GATHER_UNROLL // TOP_K):
            t = i * (GATHER_UNROLL // TOP_K) + j
            for k in range(TOP_K):
                pltpu.make_async_copy(ys_ref.at[pl.ds(dest_ref[k, t], 1)], gbuf.at[k, pl.ds(t, 1)],
                                      sem).start(priority=k % 2)
        return carry

    lax.fori_loop(0, tm * TOP_K // GATHER_UNROLL, issue, 0)
    pp = _dot(p_ref[...].astype(BF16), wpp_ref[...])
    for k in range(TOP_K):
        pltpu.make_async_copy(ys_ref.at[pl.ds(0, tm)], gbuf.at[k], sem).wait()
    gates = gate_ref[...]
    x2 = x1_ref[...]
    for k in range(TOP_K):
        x2 = x2 + gates[:, k:k + 1] * gbuf[k]
    hn = (_rms(x2, D_MODEL) * gple_ref[...]).astype(BF16)
    o_ref[...] = x2 + jax.nn.sigmoid(_dot(hn, wpg_ref[...])) * pp


def _final(dest, x1, gates_t, p, ys, wts, tile_off, n_tok, tm=TM_TOK):
    in_specs = [
        pl.BlockSpec((TOP_K, tm), lambda i: (0, i + tile_off), memory_space=pltpu.SMEM),
        pl.BlockSpec((tm, D_MODEL), lambda i: (i + tile_off, 0)),
        pl.BlockSpec((tm, TOP_K), lambda i: (i + tile_off, 0)),
        pl.BlockSpec((tm, PLE_DIM), lambda i: (i, 0)),
        pl.BlockSpec(memory_space=pl.ANY),
    ] + [_full_spec(w.shape) for w in wts]
    return pl.pallas_call(
        functools.partial(_final_body, tm=tm),
        out_shape=jax.ShapeDtypeStruct((n_tok, D_MODEL), F32),
        grid=(n_tok // tm,), in_specs=in_specs,
        out_specs=pl.BlockSpec((tm, D_MODEL), lambda i: (i, 0)),
        scratch_shapes=[pltpu.VMEM((TOP_K, tm, D_MODEL), F32), pltpu.SemaphoreType.DMA(())],
        compiler_params=pltpu.CompilerParams(dimension_semantics=("arbitrary",),
                                             vmem_limit_bytes=VMEM_LIMIT),
        name="final",
    )(dest, x1, gates_t, p, ys, *wts)


def _rope_angles(pos):
    inv = ROPE_THETA ** (-jnp.arange(ROPE_HALF, dtype=F32) / ROPE_HALF)
    ang = pos.astype(F32)[:, None] * inv
    return jnp.cos(ang), jnp.sin(ang)


def _rope_row_tables(pos):
    cos, sin = _rope_angles(pos)
    n = pos.shape[0]
    z = lambda w: jnp.zeros((n, w), F32)
    c_tab = jnp.concatenate([jnp.ones((n, QK_NOPE), F32), cos, cos, z(LANES - QK_HEAD)], axis=1)
    sa_tab = jnp.concatenate([z(QK_NOPE), -sin, z(ROPE_HALF), z(LANES - QK_HEAD)], axis=1)
    sb_tab = jnp.concatenate([z(QK_NOPE), z(ROPE_HALF), sin, z(LANES - QK_HEAD)], axis=1)
    return c_tab, sa_tab, sb_tab


def _pad_lanes(v, width=LANES):
    return jnp.pad(v, [(0, 0)] * (v.ndim - 1) + [(0, width - v.shape[-1])])


def kernel(x_prompt, x_sample, cache_ckv, cache_kpe, state_pool, page_table, p_prompt, p_sample, g_mix, w_in, g_q_a, w_q_b, g_kv_a, w_kv_b, g_q_head, g_k_head, w_pool, pool_scale, w_out, g_ffn, w_router, b_router, w_gate, b_gate, w_up, b_up, w_down, b_down, g_ple, w_ple_gate, w_ple_proj):
    assert x_prompt.shape[-1] == D_MODEL and g_mix.shape[0] == 1
    b, s, _ = x_prompt.shape
    nb, n_new, _ = x_sample.shape
    n_p, n_s = b * s, nb * n_new
    past = page_table.shape[1] * cache_ckv.shape[2]
    row = lambda v: v.reshape(1, -1).astype(F32)

    wi = w_in[0]
    w_cq, w_ckv = wi[:, :Q_LORA], wi[:, Q_LORA:Q_LORA + KV_LORA]
    w_kpe = wi[:, Q_LORA + KV_LORA:Q_LORA + KV_LORA + ROPE_DIM]
    w_u = wi[:, Q_LORA + KV_LORA + ROPE_DIM:]
    w_kpe128 = jnp.pad(w_kpe, ((0, 0), (QK_NOPE, LANES - QK_HEAD)))
    w_main = jnp.concatenate([w_cq, w_ckv, w_u, w_kpe128], axis=1).astype(BF16)
    wq = _pad_lanes(w_q_b[0]).reshape(Q_LORA, N_HEADS * LANES).astype(BF16)
    wk_nope = w_kv_b[0][:, :, :QK_NOPE]
    wk = _pad_lanes(wk_nope).reshape(KV_LORA, N_HEADS * LANES).astype(BF16)
    wv = w_kv_b[0][:, :, QK_NOPE:].reshape(KV_LORA, ATTN_WIDTH).astype(BF16)
    gq128 = _pad_lanes(row(g_q_head[0]))
    gk128 = _pad_lanes(row(g_k_head[0]))
    gkn128 = _pad_lanes(row(g_k_head[0][:QK_NOPE]))
    wkt = jnp.transpose(wk_nope, (1, 2, 0))
    wkt128 = jnp.pad(wkt, ((0, 0), (0, LANES - QK_NOPE), (0, 0))).astype(BF16)
    wkt_flat = wkt.reshape(N_HEADS * QK_NOPE, KV_LORA).astype(BF16)
    gpe_b = jnp.broadcast_to(g_k_head[0][QK_NOPE:].astype(F32)[:, None], (ROPE_DIM, LANES))
    wpool = w_pool[0].astype(BF16)
    pscale = row(pool_scale[0])
    wout_a = w_out[0][:ATTN_WIDTH].astype(BF16)
    wout_p = w_out[0][ATTN_WIDTH:].astype(BF16)
    wr_t = w_router[0].T.astype(F32)
    wr_hi = wr_t.astype(BF16)
    wr_lo = (wr_t - wr_hi.astype(F32)).astype(BF16)
    br_b = jnp.broadcast_to(b_router[0].astype(F32)[:, None], (N_EXPERTS, LANES))

    tabs_p = _rope_row_tables(jnp.arange(s, dtype=jnp.int32))
    pos_s = past + (jnp.arange(TM_PROJ, dtype=jnp.int32) % n_new)
    tabs_s = _rope_row_tables(pos_s)
    cos_t, sin_t = (t.T for t in _rope_angles(jnp.arange(past + LANES, dtype=jnp.int32)))
    tabs_attn = (cos_t[:, :past], sin_t[:, :past], cos_t[:, past:], sin_t[:, past:])

    ckv_p, kpe_p, q_p, k_p, v_p, xmid_p, pst_p = _proj_prompt(
        x_prompt, tabs_p,
        (row(g_mix[0]), w_main, row(g_q_a[0]), row(g_kv_a[0]), wq, gq128, wk, wv, gk128, wpool, pscale, wout_p))
    attn_p = _flash_prompt(q_p, k_p, v_p, b, s)

    xs_flat = x_sample.reshape(n_s, D_MODEL)
    ckv_s, kpe_s, u_s, qf_s, qlat_s = _proj_sample(
        xs_flat, tabs_s, (row(g_mix[0]), w_main, row(g_q_a[0]), row(g_kv_a[0]), wq, gq128, wkt128, gkn128))
    u_s3 = u_s.reshape(nb, n_new, POOL_WIDTH)
    pool_t = _pool_sample(jnp.transpose(state_pool[0], (1, 0, 2)), jnp.transpose(u_s3, (1, 0, 2)), wpool, pscale)
    pool_s = jnp.transpose(pool_t, (1, 0, 2)).reshape(n_s, POOL_WIDTH)
    kpe_new_t = _pad_lanes(jnp.transpose(kpe_s.reshape(nb, n_new, ROPE_DIM), (0, 2, 1)))
    attn_s = _sample_attn(page_table, cache_ckv[0], jnp.transpose(cache_kpe[0], (0, 2, 1)), qlat_s, qf_s,
                          ckv_s.reshape(nb, n_new, KV_LORA), kpe_new_t,
                          tabs_attn, (wkt_flat, gpe_b, wv)).reshape(n_s, ATTN_WIDTH)

    x1, h2, topi, gates, rank, counts = _post_attn(
        xmid_p, attn_p, xs_flat, attn_s, pool_s, (wout_a, wout_p, row(g_ffn[0]), wr_hi, wr_lo, br_b))
    nt = n_p + n_s
    counts = counts[:, 0]
    padded = (counts + TB - 1) // TB * TB
    ends = jnp.cumsum(padded)
    starts = ends - padded
    eids = jnp.arange(N_EXPERTS, dtype=jnp.int32)
    dest = (jnp.sum(jnp.where(topi[..., None] == eids, starts, 0), axis=-1) + rank).astype(jnp.int32)
    n_rows = nt * TOP_K + N_EXPERTS * TB
    n_blocks = n_rows // TB
    n_used = (ends[-1] // TB).astype(jnp.int32).reshape(1)
    blk_start = jnp.arange(n_blocks, dtype=jnp.int32) * TB
    blk_start = jnp.minimum(blk_start, jnp.maximum(ends[-1] - TB, 0))
    block_expert = jnp.minimum(jnp.sum((ends[None, :] <= blk_start[:, None]).astype(jnp.int32), axis=1),
                               N_EXPERTS - 1).astype(jnp.int32)

    row_token = _invert(dest, n_rows)
    b3 = lambda v: v[0].reshape(N_EXPERTS, 1, D_MODEL).astype(F32)
    ys = _experts(block_expert, n_used, row_token, h2, w_gate[0].astype(BF16), b3(b_gate), w_up[0].astype(BF16),
                  b3(b_up), w_down[0].astype(BF16), b3(b_down))
    gates_t = gates.T
    fin_w = (row(g_ple[0]), w_ple_gate[0].astype(BF16), w_ple_proj[0].astype(BF16))
    y_p = _final(dest, x1, gates_t, p_prompt[0].reshape(n_p, PLE_DIM), ys, fin_w, 0, n_p)
    y_s = _final(dest, x1, gates_t, p_sample[0].reshape(n_s, PLE_DIM), ys, fin_w, n_p // TM_TOK, n_s)

    pool_state_s = jnp.concatenate([state_pool[0][:, n_new:], u_s3], axis=1)
    return (y_p.reshape(b, s, D_MODEL), y_s.reshape(nb, n_new, D_MODEL),
            ckv_p.reshape(1, b, s, KV_LORA), kpe_p.reshape(1, b, s, ROPE_DIM), pst_p[:, HALO - POOL_PREV:][None],
            ckv_s.reshape(1, nb, n_new, KV_LORA), kpe_s.reshape(1, nb, n_new, ROPE_DIM), pool_state_s[None])
```

```python
import functools

import jax
import jax.numpy as jnp
from jax import lax
from jax.experimental import pallas as pl
from jax.experimental.pallas import tpu as pltpu

F32 = jnp.float32
BF16 = jnp.bfloat16

D_MODEL = 1024
N_HEADS = 8
QK_NOPE = 64
ROPE_DIM = 32
ROPE_HALF = ROPE_DIM // 2
QK_HEAD = QK_NOPE + ROPE_DIM
V_HEAD = 64
Q_LORA = 384
KV_LORA = 256
ATTN_WIDTH = N_HEADS * V_HEAD
ROPE_THETA = 10000.0
SM_SCALE = QK_HEAD ** -0.5
POOL_WINDOWS = (2, 4, 8, 16)
POOL_GROUP_DIM = 128
POOL_WIDTH = 512
POOL_PREV = 15
N_EXPERTS = 32
TOP_K = 4
SWIGLU_LIMIT = 7.0
SWIGLU_ALPHA = 1.702
PLE_DIM = 256
EPS = 1e-6

LANES = 128
HALO = 16
MAIN_COLS = Q_LORA + KV_LORA + POOL_WIDTH + LANES
VMEM_LIMIT = 56 * 1024 * 1024

TM_PROJ = 256
TQ, TK = 256, 1024
PAGES_PER_STEP = 16
RING = 3
SUB = 256
TM_TOK = 256
TM_DISP = 512
TB = 256
GATHER_UNROLL = 8

NEG_INF = float("-inf")


def _dot(a, b):
    return jnp.dot(a, b, preferred_element_type=F32)


def _dot_nt(a, b):
    return lax.dot_general(a, b, (((1,), (1,)), ((), ())), preferred_element_type=F32)


def _rms(x, n):
    return x * lax.rsqrt(jnp.sum(x * x, axis=-1, keepdims=True) * (1.0 / n) + EPS)


def _rope_rows(x, c_tab, sa_tab, sb_tab):
    return (x * c_tab + pltpu.roll(x, LANES - ROPE_HALF, 1) * sa_tab
            + pltpu.roll(x, ROPE_HALF, 1) * sb_tab)


def _proj_body(*refs, tm, seq_mode):
    if seq_mode:
        (x_ref, c_ref, sa_ref, sb_ref, gmix_ref, wmain_ref, gqa_ref, gkva_ref, wq_ref, gq_ref,
         wk_ref, wv_ref, gk_ref, wpool_ref, pscale_ref, woutp_ref,
         ckv_ref, kpe_ref, q_ref, k_ref, v_ref, xmid_ref, pst_ref, ubuf) = refs
    else:
        (x_ref, c_ref, sa_ref, sb_ref, gmix_ref, wmain_ref, gqa_ref, gkva_ref, wq_ref, gq_ref,
         wkt_ref, gkn_ref,
         ckv_ref, kpe_ref, u_ref, q_ref, qlat_ref) = refs

    x = x_ref[...]
    h = (_rms(x, D_MODEL) * gmix_ref[...]).astype(BF16)
    z = _dot(h, wmain_ref[...])
    cqn = (_rms(z[:, :Q_LORA], Q_LORA) * gqa_ref[...]).astype(BF16)
    ckvn = _rms(z[:, Q_LORA:Q_LORA + KV_LORA], KV_LORA) * gkva_ref[...]
    u = z[:, Q_LORA + KV_LORA:Q_LORA + KV_LORA + POOL_WIDTH]
    kpe128 = z[:, Q_LORA + KV_LORA + POOL_WIDTH:]
    ckv_ref[...] = ckvn
    kpe_ref[...] = pltpu.roll(kpe128, LANES - QK_NOPE, 1)[:, :ROPE_DIM]

    c_tab, sa_tab, sb_tab = c_ref[...], sa_ref[...], sb_ref[...]
    q_all = _dot(cqn, wq_ref[...])
    gq = gq_ref[...]
    for hd in range(N_HEADS):
        qh = q_all[:, hd * LANES:(hd + 1) * LANES]
        qh = _rope_rows(_rms(qh, QK_HEAD) * gq, c_tab, sa_tab, sb_tab) * SM_SCALE
        if seq_mode:
            q_ref[hd] = qh.astype(BF16)
        else:
            q_ref[hd] = qh
            qlat_ref[hd] = _dot((qh * gkn_ref[...]).astype(BF16), wkt_ref[hd])

    if not seq_mode:
        u_ref[...] = u
        return

    ckvb = ckvn.astype(BF16)
    k_all = _dot(ckvb, wk_ref[...])
    gk = gk_ref[...]
    for hd in range(N_HEADS):
        kh = k_all[:, hd * LANES:(hd + 1) * LANES] + kpe128
        kh = _rope_rows(_rms(kh, QK_HEAD) * gk, c_tab, sa_tab, sb_tab)
        k_ref[hd] = kh.astype(BF16)
    v_all = _dot(ckvb, wv_ref[...])
    for hp in range(N_HEADS // 2):
        v_ref[hp] = v_all[:, hp * LANES:(hp + 1) * LANES].astype(BF16)

    j = pl.program_id(1)

    @pl.when(j == 0)
    def _():
        ubuf[0:HALO, :] = jnp.zeros((HALO, POOL_WIDTH), F32)

    ubuf[HALO:HALO + tm, :] = u
    pos1 = (j * tm + 1 + lax.broadcasted_iota(jnp.int32, (tm, 1), 0)).astype(F32)
    pool_parts = []
    for g, w in enumerate(POOL_WINDOWS):
        sl = slice(g * POOL_GROUP_DIM, (g + 1) * POOL_GROUP_DIM)
        ug = u[:, sl]
        acc = ug
        for back in range(1, w):
            acc = acc + ubuf[HALO - back:HALO - back + tm, sl]
        d = acc / jnp.minimum(float(w), pos1) - ug
        pool_parts.append(_dot(d.astype(BF16), wpool_ref[g]) * pscale_ref[:, sl])
    pool_out = jnp.concatenate(pool_parts, axis=1).astype(BF16)
    xmid_ref[...] = x + _dot(pool_out, woutp_ref[...])

    tail = ubuf[tm:tm + HALO, :]
    ubuf[0:HALO, :] = tail

    @pl.when(j == pl.num_programs(1) - 1)
    def _():
        pst_ref[...] = tail


def _full_spec(shape):
    nd = len(shape)
    return pl.BlockSpec(shape, lambda *_: (0,) * nd)


def _proj_prompt(x, tabs, wts, tm=TM_PROJ):
    b, s, _ = x.shape
    n = b * s
    nj = s // tm
    c_tab, sa_tab, sb_tab = tabs
    row = lambda width: pl.BlockSpec((tm, width), lambda bi, j: (bi * nj + j, 0))
    tab = pl.BlockSpec((tm, LANES), lambda bi, j: (j, 0))
    heads = lambda nh, width: pl.BlockSpec((nh, tm, width), lambda bi, j: (0, bi * nj + j, 0))
    in_specs = [pl.BlockSpec((None, tm, D_MODEL), lambda bi, j: (bi, j, 0)), tab, tab, tab] + [
        _full_spec(w.shape) for w in wts]
    out_shape = (
        jax.ShapeDtypeStruct((n, KV_LORA), F32),
        jax.ShapeDtypeStruct((n, ROPE_DIM), F32),
        jax.ShapeDtypeStruct((N_HEADS, n, LANES), BF16),
        jax.ShapeDtypeStruct((N_HEADS, n, LANES), BF16),
        jax.ShapeDtypeStruct((N_HEADS // 2, n, LANES), BF16),
        jax.ShapeDtypeStruct((n, D_MODEL), F32),
        jax.ShapeDtypeStruct((b, HALO, POOL_WIDTH), F32),
    )
    out_specs = (row(KV_LORA), row(ROPE_DIM), heads(N_HEADS, LANES), heads(N_HEADS, LANES),
                 heads(N_HEADS // 2, LANES), row(D_MODEL),
                 pl.BlockSpec((None, HALO, POOL_WIDTH), lambda bi, j: (bi, 0, 0)))
    return pl.pallas_call(
        functools.partial(_proj_body, tm=tm, seq_mode=True),
        out_shape=out_shape, grid=(b, nj), in_specs=in_specs, out_specs=out_specs,
        scratch_shapes=[pltpu.VMEM((tm + HALO, POOL_WIDTH), F32)],
        compiler_params=pltpu.CompilerParams(dimension_semantics=("arbitrary", "arbitrary"),
                                             vmem_limit_bytes=VMEM_LIMIT),
        name="proj_prompt",
    )(x, c_tab, sa_tab, sb_tab, *wts)


def _proj_sample(x, tabs, wts, tm=TM_PROJ):
    n = x.shape[0]
    c_tab, sa_tab, sb_tab = tabs
    row = lambda width: pl.BlockSpec((tm, width), lambda i: (i, 0))
    heads = lambda width: pl.BlockSpec((N_HEADS, tm, width), lambda i: (0, i, 0))
    in_specs = [row(D_MODEL), _full_spec((tm, LANES)), _full_spec((tm, LANES)), _full_spec((tm, LANES))] + [
        _full_spec(w.shape) for w in wts]
    out_shape = (
        jax.ShapeDtypeStruct((n, KV_LORA), F32),
        jax.ShapeDtypeStruct((n, ROPE_DIM), F32),
        jax.ShapeDtypeStruct((n, POOL_WIDTH), F32),
        jax.ShapeDtypeStruct((N_HEADS, n, LANES), F32),
        jax.ShapeDtypeStruct((N_HEADS, n, KV_LORA), F32),
    )
    out_specs = (row(KV_LORA), row(ROPE_DIM), row(POOL_WIDTH), heads(LANES), heads(KV_LORA))
    return pl.pallas_call(
        functools.partial(_proj_body, tm=tm, seq_mode=False),
        out_shape=out_shape, grid=(n // tm,), in_specs=in_specs, out_specs=out_specs,
        compiler_params=pltpu.CompilerParams(dimension_semantics=("arbitrary",),
                                             vmem_limit_bytes=VMEM_LIMIT),
        name="proj_sample",
    )(x, c_tab, sa_tab, sb_tab, *wts)


def _flash_body(q_ref, k_ref, v_ref, o_ref, m_sc, l_sc, acc_sc, *, tq, tk):
    qi = pl.program_id(2)
    n_full = (qi * tq) // tk
    row = qi * tq + lax.broadcasted_iota(jnp.int32, (tq, tk), 0)
    col = lax.broadcasted_iota(jnp.int32, (tq, tk), 1)
    m_sc[...] = jnp.full(m_sc.shape, NEG_INF, F32)
    l_sc[...] = jnp.zeros(l_sc.shape, F32)
    acc_sc[...] = jnp.zeros(acc_sc.shape, F32)

    def step(ki, masked):
        start = pl.multiple_of(ki * tk, tk)
        v = v_ref[pl.ds(start, tk), :]
        for hh in range(2):
            s = _dot_nt(q_ref[hh], k_ref[hh, pl.ds(start, tk), :])
            if masked:
                s = jnp.where(col + start <= row, s, NEG_INF)
            m_prev = m_sc[hh]
            m_new = jnp.maximum(m_prev, jnp.max(s, axis=-1, keepdims=True))
            alpha = jnp.exp(m_prev - m_new)
            p = jnp.exp(s - m_new)
            l_sc[hh] = alpha * l_sc[hh] + jnp.sum(p, axis=-1, keepdims=True)
            acc_sc[hh] = alpha * acc_sc[hh] + _dot(p.astype(BF16), v)
            m_sc[hh] = m_new

    def loop_body(ki, carry):
        step(ki, False)
        return carry

    lax.fori_loop(0, n_full, loop_body, 0)
    step(n_full, True)
    lane = lax.broadcasted_iota(jnp.int32, (tq, LANES), 1)
    o_ref[...] = jnp.where(lane < V_HEAD, acc_sc[0] / l_sc[0], acc_sc[1] / l_sc[1]).astype(o_ref.dtype)


def _flash_prompt(q, k, v, b, s, tq=TQ, tk=TK):
    n = b * s
    nq = s // tq
    return pl.pallas_call(
        functools.partial(_flash_body, tq=tq, tk=tk),
        out_shape=jax.ShapeDtypeStruct((n, ATTN_WIDTH), BF16),
        grid=(b, N_HEADS // 2, nq),
        in_specs=[
            pl.BlockSpec((2, tq, LANES), lambda bi, hp, qi: (hp, bi * nq + qi, 0)),
            pl.BlockSpec((2, s, LANES), lambda bi, hp, qi: (hp, bi, 0)),
            pl.BlockSpec((None, s, LANES), lambda bi, hp, qi: (hp, bi, 0)),
        ],
        out_specs=pl.BlockSpec((tq, LANES), lambda bi, hp, qi: (bi * nq + qi, hp)),
        scratch_shapes=[pltpu.VMEM((2, tq, 1), F32), pltpu.VMEM((2, tq, 1), F32), pltpu.VMEM((2, tq, LANES), F32)],
        compiler_params=pltpu.CompilerParams(dimension_semantics=("arbitrary",) * 3,
                                             vmem_limit_bytes=VMEM_LIMIT),
        name="flash_prompt",
    )(q, k, v)


def _pool_sample_body(st_ref, u_ref, wpool_ref, pscale_ref, o_ref, *, n_new):
    for t in range(n_new):
        for g, w in enumerate(POOL_WINDOWS):
            sl = slice(g * POOL_GROUP_DIM, (g + 1) * POOL_GROUP_DIM)
            ug = u_ref[t, :, sl]
            acc = ug
            for back in range(1, w):
                src = t - back
                acc = acc + (u_ref[src, :, sl] if src >= 0 else st_ref[POOL_PREV + src, :, sl])
            d = acc / float(w) - ug
            o_ref[t, :, sl] = _dot(d.astype(BF16), wpool_ref[g]) * pscale_ref[:, sl]


def _pool_sample(state_t, u_t, wpool, pscale):
    n_new, nb, _ = u_t.shape
    return pl.pallas_call(
        functools.partial(_pool_sample_body, n_new=n_new),
        out_shape=jax.ShapeDtypeStruct((n_new, nb, POOL_WIDTH), F32),
        grid=(1,),
        in_specs=[_full_spec(state_t.shape), _full_spec(u_t.shape), _full_spec(wpool.shape),
                  _full_spec(pscale.shape)],
        out_specs=_full_spec((n_new, nb, POOL_WIDTH)),
        compiler_params=pltpu.CompilerParams(dimension_semantics=("arbitrary",),
                                             vmem_limit_bytes=VMEM_LIMIT),
        name="pool_sample",
    )(state_t, u_t, wpool, pscale)


def _sample_attn_body(pt_ref, ckv_hbm, kpe_hbm, qlat_ref, q_ref, ckvn_ref, kpen_ref, cos_ref, sin_ref,
                      cost_ref, sint_ref, wkt_ref, gpe_ref, wv_ref, o_ref,
                      cbuf, kbuf, sem_c, sem_k, a_sc, m_sc, l_sc, acc_sc, cbb0, cbb1, sb0, sb1,
                      *, n_pages, n_new):
    npg = PAGES_PER_STEP
    page = LANES
    n_chunks = n_pages // npg
    b = pl.program_id(0)
    nb = pl.num_programs(0)
    rows = N_HEADS * n_new

    def page_copies(seq, chunk, slot):
        cps = []
        for i in range(npg):
            pg = pt_ref[seq, chunk * npg + i]
            cps.append(pltpu.make_async_copy(ckv_hbm.at[pg], cbuf.at[slot, pl.ds(i * page, page)], sem_c.at[slot]))
            cps.append(pltpu.make_async_copy(kpe_hbm.at[pg], kbuf.at[slot, i], sem_k.at[slot]))
        return cps

    def start_chunk(seq, chunk, slot):
        for cp in page_copies(seq, chunk, slot):
            cp.start()

    def wait_chunk(slot):
        for cp in page_copies(0, 0, slot):
            cp.wait()

    total = nb * n_chunks

    @pl.when(b == 0)
    def _():
        for g0 in range(RING - 1):
            start_chunk(g0 // n_chunks, g0 % n_chunks, g0 % RING)

    a_sc[0:N_HEADS * QK_NOPE, :] = wkt_ref[...]
    a_sc[N_HEADS * QK_NOPE:, :] = qlat_ref[...].reshape(rows, KV_LORA).astype(BF16)
    m_sc[...] = jnp.full((rows, LANES), NEG_INF, F32)
    l_sc[...] = jnp.zeros((rows, LANES), F32)
    acc_sc[...] = jnp.zeros((rows, KV_LORA), F32)
    qpe = q_ref[...].reshape(rows, LANES)[:, QK_NOPE:QK_HEAD].astype(BF16)
    g1 = gpe_ref[0:ROPE_HALF, :]
    g2 = gpe_ref[ROPE_HALF:ROPE_DIM, :]

    def scores(cb, kpt, cos, sin):
        ck = cb.shape[0]
        r = _dot_nt(a_sc[...], cb)
        kt = r[:N_HEADS * QK_NOPE]
        ssn = jnp.sum((kt * kt).reshape(N_HEADS, QK_NOPE, ck), axis=1)
        x1 = kpt[0:ROPE_HALF]
        x2 = kpt[ROPE_HALF:ROPE_DIM]
        ssp = jnp.sum(x1 * x1 + x2 * x2, axis=0, keepdims=True)
        rs = lax.rsqrt((ssn + ssp) * (1.0 / QK_HEAD) + EPS)
        reps = ck // LANES
        x1 = x1 * jnp.tile(g1, (1, reps))
        x2 = x2 * jnp.tile(g2, (1, reps))
        rot = jnp.concatenate([x1 * cos - x2 * sin, x2 * cos + x1 * sin], axis=0).astype(BF16)
        s = r[N_HEADS * QK_NOPE:] + _dot(qpe, rot)
        s = s.reshape(N_HEADS, n_new, ck) * rs[:, None, :]
        return s.reshape(rows, ck)

    def update(s_all, cbs):
        m_prev = m_sc[:, :1]
        m_new = jnp.maximum(m_prev, jnp.max(s_all, axis=-1, keepdims=True))
        alpha = jnp.exp(m_prev - m_new)
        p = jnp.exp(s_all - m_new)
        l_sc[...] = jnp.broadcast_to(alpha * l_sc[:, :1] + jnp.sum(p, axis=-1, keepdims=True),
                                     (rows, LANES))
        acc = alpha * acc_sc[...]
        off = 0
        for cb in cbs:
            ck = cb.shape[0]
            acc = acc + _dot(p[:, off:off + ck].astype(BF16), cb)
            off += ck
        acc_sc[...] = acc
        m_sc[...] = jnp.broadcast_to(m_new, (rows, LANES))

    pages_per_sub = SUB // page

    n_sub = npg // pages_per_sub

    def scores_chunk(jc, cbb, s_buf):
        g = b * n_chunks + jc
        slot = g % RING
        wait_chunk(slot)
        nxt = jnp.minimum(g + RING - 1, total - 1)
        start_chunk(nxt // n_chunks, nxt % n_chunks, (g + RING - 1) % RING)
        for sc in range(n_sub):
            cb = cbuf[slot, sc * SUB:(sc + 1) * SUB, :].astype(BF16)
            cbb[sc * SUB:(sc + 1) * SUB, :] = cb
            kpt = jnp.concatenate([kbuf[slot, sc * pages_per_sub + i] for i in range(pages_per_sub)], axis=1)
            start = pl.multiple_of(jc * (npg * page) + sc * SUB, SUB)
            s_buf[:, sc * SUB:(sc + 1) * SUB] = scores(
                cb, kpt, cos_ref[:, pl.ds(start, SUB)], sin_ref[:, pl.ds(start, SUB)])

    def update_chunk(cbb, s_buf):
        update(s_buf[...], [cbb[sc * SUB:(sc + 1) * SUB, :] for sc in range(n_sub)])

    scores_chunk(0, cbb0, sb0)

    def pair(i, carry):
        scores_chunk(2 * i + 1, cbb1, sb1)
        update_chunk(cbb0, sb0)
        scores_chunk(2 * i + 2, cbb0, sb0)
        update_chunk(cbb1, sb1)
        return carry

    lax.fori_loop(0, n_chunks // 2 - 1, pair, 0)
    scores_chunk(n_chunks - 1, cbb1, sb1)
    update_chunk(cbb0, sb0)
    update_chunk(cbb1, sb1)

    @pl.when(b == nb - 1)
    def _():
        for extra in range(RING - 1):
            wait_chunk((total + extra) % RING)

    pad = LANES - n_new
    cb = jnp.concatenate([ckvn_ref[...], jnp.zeros((pad, KV_LORA), F32)], axis=0).astype(BF16)
    s = scores(cb, kpen_ref[...], cost_ref[...], sint_ref[...])
    key = lax.broadcasted_iota(jnp.int32, (rows, LANES), 1)
    tok = lax.broadcasted_iota(jnp.int32, (rows, LANES), 0) % n_new
    update(jnp.where(key <= tok, s, NEG_INF), [cb])
    o_lat = (acc_sc[...] / l_sc[:, :1]).astype(BF16)
    full = _dot(o_lat, wv_ref[...]).reshape(N_HEADS, n_new, ATTN_WIDTH)
    hd = lax.broadcasted_iota(jnp.int32, (N_HEADS, n_new, ATTN_WIDTH), 0)
    colh = lax.broadcasted_iota(jnp.int32, (N_HEADS, n_new, ATTN_WIDTH), 2) // V_HEAD
    o_ref[...] = jnp.sum(jnp.where(hd == colh, full, 0.0), axis=0)


def _sample_attn(page_table, cache_ckv, cache_kpe_t, qlat, qf, ckv_new, kpe_new_t, tabs, wts):
    nb, n_pages = page_table.shape
    page = cache_ckv.shape[1]
    n_new = ckv_new.shape[1]
    npg = PAGES_PER_STEP
    assert page == LANES and n_pages % (2 * npg) == 0 and nb * (n_pages // npg) >= RING
    cos_t, sin_t, cos_tail, sin_tail = tabs
    rows = N_HEADS * n_new
    const = lambda shape: pl.BlockSpec(shape, lambda b, pt, nd=len(shape): (0,) * nd)
    in_specs = [
        pl.BlockSpec(memory_space=pl.ANY), pl.BlockSpec(memory_space=pl.ANY),
        pl.BlockSpec((N_HEADS, n_new, KV_LORA), lambda b, pt: (0, b, 0)),
        pl.BlockSpec((N_HEADS, n_new, LANES), lambda b, pt: (0, b, 0)),
        pl.BlockSpec((None, n_new, KV_LORA), lambda b, pt: (b, 0, 0)),
        pl.BlockSpec((None, ROPE_DIM, LANES), lambda b, pt: (b, 0, 0)),
        const(cos_t.shape), const(sin_t.shape), const(cos_tail.shape), const(sin_tail.shape),
    ] + [const(w.shape) for w in wts]
    grid_spec = pltpu.PrefetchScalarGridSpec(
        num_scalar_prefetch=1, grid=(nb,), in_specs=in_specs,
        out_specs=pl.BlockSpec((None, n_new, ATTN_WIDTH), lambda b, pt: (b, 0, 0)),
        scratch_shapes=[pltpu.VMEM((RING, npg * page, KV_LORA), F32),
                        pltpu.VMEM((RING, npg, ROPE_DIM, page), F32),
                        pltpu.SemaphoreType.DMA((RING,)), pltpu.SemaphoreType.DMA((RING,)),
                        pltpu.VMEM((N_HEADS * QK_NOPE + rows, KV_LORA), BF16),
                        pltpu.VMEM((rows, LANES), F32), pltpu.VMEM((rows, LANES), F32),
                        pltpu.VMEM((rows, KV_LORA), F32),
                        pltpu.VMEM((npg * page, KV_LORA), BF16), pltpu.VMEM((npg * page, KV_LORA), BF16),
                        pltpu.VMEM((rows, npg * page), F32), pltpu.VMEM((rows, npg * page), F32)])
    return pl.pallas_call(
        functools.partial(_sample_attn_body, n_pages=n_pages, n_new=n_new),
        out_shape=jax.ShapeDtypeStruct((nb, n_new, ATTN_WIDTH), F32),
        grid_spec=grid_spec,
        compiler_params=pltpu.CompilerParams(dimension_semantics=("arbitrary",),
                                             vmem_limit_bytes=VMEM_LIMIT),
        name="sample_attn",
    )(page_table, cache_ckv, cache_kpe_t, qlat, qf, ckv_new, kpe_new_t, cos_t, sin_t, cos_tail, sin_tail, *wts)


def _post_attn_body(xmid_ref, attn_p_ref, xs_ref, attn_s_ref, pool_s_ref, wouta_ref, woutp_ref, gffn_ref,
                    wrh_ref, wrl_ref, br_ref,
                    x1_ref, h2_ref, topi_ref, gate_ref, rank_ref, cnt_ref, base_sc, xm_sc, at_sc,
                    *, tm, n_prompt_tiles):
    i = pl.program_id(0)

    @pl.when(i == 0)
    def _():
        base_sc[...] = jnp.zeros((N_EXPERTS, LANES), F32)

    @pl.when(i < n_prompt_tiles)
    def _():
        xm_sc[...] = xmid_ref[...]
        at_sc[...] = attn_p_ref[...]

    @pl.when(i >= n_prompt_tiles)
    def _():
        xm_sc[...] = xs_ref[...] + _dot(pool_s_ref[...].astype(BF16), woutp_ref[...])
        at_sc[...] = attn_s_ref[...].astype(BF16)

    x1 = xm_sc[...] + _dot(at_sc[...], wouta_ref[...])
    x1_ref[...] = x1
    h2 = _rms(x1, D_MODEL) * gffn_ref[...]
    h2_ref[...] = h2
    h_hi = h2.astype(BF16)
    h_lo = (h2 - h_hi.astype(F32)).astype(BF16)
    lg = (_dot_nt(wrh_ref[...], h_hi) + _dot_nt(wrh_ref[...], h_lo) + _dot_nt(wrl_ref[...], h_hi)
          + br_ref[:, :1])
    eid = lax.broadcasted_iota(jnp.int32, (N_EXPERTS, tm), 0)
    vals, idxs, hots = [], [], []
    for _ in range(TOP_K):
        mx = jnp.max(lg, axis=0, keepdims=True)
        ix = jnp.min(jnp.where(lg == mx, eid, N_EXPERTS), axis=0, keepdims=True)
        hot = eid == ix
        lg = jnp.where(hot, NEG_INF, lg)
        vals.append(mx)
        idxs.append(ix)
        hots.append(hot)
    ex = [jnp.exp(v - vals[0]) for v in vals]
    den = ex[0] + ex[1] + ex[2] + ex[3]
    gate_ref[...] = jnp.concatenate([e / den for e in ex], axis=0)
    topi_ref[...] = jnp.concatenate(idxs, axis=0)

    anyhot = (hots[0] | hots[1] | hots[2] | hots[3])
    hot_f = jnp.where(anyhot, 1.0, 0.0)
    upper = jnp.where(lax.broadcasted_iota(jnp.int32, (tm, tm), 0) < lax.broadcasted_iota(jnp.int32, (tm, tm), 1),
                      1.0, 0.0).astype(BF16)
    before = _dot(hot_f.astype(BF16), upper) + base_sc[:, :1]
    ranks = [jnp.sum(jnp.where(h, before, 0.0), axis=0, keepdims=True) for h in hots]
    rank_ref[...] = jnp.concatenate(ranks, axis=0).astype(jnp.int32)
    base_sc[...] = base_sc[...] + jnp.sum(hot_f, axis=1, keepdims=True)
    cnt_ref[...] = base_sc[...].astype(jnp.int32)


def _post_attn(xmid_p, attn_p, x_s, attn_s, pool_s, wts, tm=TM_TOK):
    n_p, n_s = xmid_p.shape[0], x_s.shape[0]
    npt, nst = n_p // tm, n_s // tm
    nt = n_p + n_s
    p_idx = lambda i: (jnp.minimum(i, npt - 1), 0)
    s_idx = lambda i: (jnp.maximum(i - npt, 0), 0)
    in_specs = [
        pl.BlockSpec((tm, D_MODEL), p_idx), pl.BlockSpec((tm, ATTN_WIDTH), p_idx),
        pl.BlockSpec((tm, D_MODEL), s_idx), pl.BlockSpec((tm, ATTN_WIDTH), s_idx),
        pl.BlockSpec((tm, POOL_WIDTH), s_idx),
    ] + [_full_spec(w.shape) for w in wts]
    col = lambda rows: pl.BlockSpec((rows, tm), lambda i: (0, i))
    out_shape = (
        jax.ShapeDtypeStruct((nt, D_MODEL), F32), jax.ShapeDtypeStruct((nt, D_MODEL), F32),
        jax.ShapeDtypeStruct((TOP_K, nt), jnp.int32), jax.ShapeDtypeStruct((TOP_K, nt), F32),
        jax.ShapeDtypeStruct((TOP_K, nt), jnp.int32), jax.ShapeDtypeStruct((N_EXPERTS, LANES), jnp.int32),
    )
    out_specs = (pl.BlockSpec((tm, D_MODEL), lambda i: (i, 0)), pl.BlockSpec((tm, D_MODEL), lambda i: (i, 0)),
                 col(TOP_K), col(TOP_K), col(TOP_K), _full_spec((N_EXPERTS, LANES)))
    return pl.pallas_call(
        functools.partial(_post_attn_body, tm=tm, n_prompt_tiles=npt),
        out_shape=out_shape, grid=(npt + nst,), in_specs=in_specs, out_specs=out_specs,
        scratch_shapes=[pltpu.VMEM((N_EXPERTS, LANES), F32), pltpu.VMEM((tm, D_MODEL), F32),
                        pltpu.VMEM((tm, ATTN_WIDTH), BF16)],
        compiler_params=pltpu.CompilerParams(dimension_semantics=("arbitrary",),
                                             vmem_limit_bytes=VMEM_LIMIT),
        name="post_attn",
    )(xmid_p, attn_p, x_s, attn_s, pool_s, *wts)


def _invert_body(dest_ref, zeros_hbm, rt_ref, sem, *, tm):
    i = pl.program_id(0)

    @pl.when(i == 0)
    def _():
        cp = pltpu.make_async_copy(zeros_hbm, rt_ref, sem)
        cp.start()
        cp.wait()

    base = i * tm

    def body(i, carry):
        for j in range(GATHER_UNROLL // TOP_K):
            t = i * (GATHER_UNROLL // TOP_K) + j
            for k in range(TOP_K):
                rt_ref[dest_ref[k, t]] = base + t
        return carry

    lax.fori_loop(0, tm * TOP_K // GATHER_UNROLL, body, 0)


def _invert(dest, n_rows, tm=TM_DISP):
    nt = dest.shape[1]
    return pl.pallas_call(
        functools.partial(_invert_body, tm=tm),
        out_shape=jax.ShapeDtypeStruct((n_rows,), jnp.int32),
        grid=(nt // tm,),
        in_specs=[pl.BlockSpec((TOP_K, tm), lambda i: (0, i), memory_space=pltpu.SMEM),
                  pl.BlockSpec(memory_space=pl.ANY)],
        out_specs=pl.BlockSpec(memory_space=pltpu.SMEM),
        scratch_shapes=[pltpu.SemaphoreType.DMA(())],
        compiler_params=pltpu.CompilerParams(dimension_semantics=("arbitrary",)),
        name="invert",
    )(dest, jnp.zeros((n_rows,), jnp.int32))


def _experts_body(be_ref, nu_ref, rt_cur_ref, rt_nxt_ref, h2_ref, wg_ref, bg_ref, wu_ref, bu_ref, wd_ref, bd_ref,
                  y_ref, xbuf, sem, wgb, wub, wdb, *, tb):
    blk = pl.program_id(0)
    slot = blk % 2
    n_used = nu_ref[0]

    @pl.when((blk == 0) | (be_ref[blk] != be_ref[jnp.maximum(blk - 1, 0)]))
    def _():
        wgb[...] = wg_ref[...].astype(BF16)
        wub[...] = wu_ref[...].astype(BF16)
        wdb[...] = wd_ref[...].astype(BF16)

    def gather(rt_ref, s):
        def body(i, carry):
            for j in range(GATHER_UNROLL):
                r = i * GATHER_UNROLL + j
                pltpu.make_async_copy(h2_ref.at[pl.ds(rt_ref[r], 1)], xbuf.at[s, pl.ds(r, 1)],
                                      sem.at[s]).start(priority=j % 2)
            return carry
        lax.fori_loop(0, tb // GATHER_UNROLL, body, 0)

    @pl.when((blk == 0) & (n_used > 0))
    def _():
        gather(rt_cur_ref, 0)

    @pl.when(blk + 1 < n_used)
    def _():
        gather(rt_nxt_ref, 1 - slot)

    @pl.when(blk < n_used)
    def _():
        pltpu.make_async_copy(h2_ref.at[pl.ds(0, tb)], xbuf.at[slot], sem.at[slot]).wait()
        xb = xbuf[slot].astype(BF16)
        g = jnp.minimum(_dot(xb, wgb[...]) + bg_ref[...], SWIGLU_LIMIT)
        u = jnp.clip(_dot(xb, wub[...]) + bu_ref[...], -SWIGLU_LIMIT, SWIGLU_LIMIT)
        act = (u + 1.0) * (g * jax.nn.sigmoid(SWIGLU_ALPHA * g))
        y_ref[...] = _dot(act.astype(BF16), wdb[...]) + bd_ref[...]

    @pl.when(blk >= n_used)
    def _():
        y_ref[...] = jnp.zeros(y_ref.shape, F32)


def _experts(block_expert, n_used, row_token, h2, wg, bg, wu, bu, wd, bd, tb=TB):
    r = row_token.shape[0]
    nblk = r // tb
    w_spec = pl.BlockSpec((None, D_MODEL, D_MODEL), lambda b, be, nu: (be[b], 0, 0))
    b_spec = pl.BlockSpec((None, 1, D_MODEL), lambda b, be, nu: (be[b], 0, 0))
    rt_cur = pl.BlockSpec((tb,), lambda b, be, nu: (b,), memory_space=pltpu.SMEM)
    rt_nxt = pl.BlockSpec((tb,), lambda b, be, nu: (jnp.minimum(b + 1, nblk - 1),), memory_space=pltpu.SMEM)
    grid_spec = pltpu.PrefetchScalarGridSpec(
        num_scalar_prefetch=2, grid=(nblk,),
        in_specs=[rt_cur, rt_nxt, pl.BlockSpec(memory_space=pl.ANY),
                  w_spec, b_spec, w_spec, b_spec, w_spec, b_spec],
        out_specs=pl.BlockSpec((tb, D_MODEL), lambda b, be, nu: (b, 0)),
        scratch_shapes=[pltpu.VMEM((2, tb, D_MODEL), F32), pltpu.SemaphoreType.DMA((2,))]
        + [pltpu.VMEM((D_MODEL, D_MODEL), BF16)] * 3)
    return pl.pallas_call(
        functools.partial(_experts_body, tb=tb),
        out_shape=jax.ShapeDtypeStruct((r, D_MODEL), F32), grid_spec=grid_spec,
        compiler_params=pltpu.CompilerParams(dimension_semantics=("arbitrary",),
                                             vmem_limit_bytes=VMEM_LIMIT),
        name="experts",
    )(block_expert, n_used, row_token, row_token, h2, wg, bg, wu, bu, wd, bd)


def _final_body(dest_ref, x1_ref, gate_ref, p_ref, ys_ref, gple_ref, wpg_ref, wpp_ref, o_ref, gbuf, sem, *, tm):
    def issue(i, carry):
        for j in range(GATHER_UNROLL // TOP_K):
            t = i * (GATHER_UNROLL // TOP_K) + j
            for k in range(TOP_K):
                pltpu.make_async_copy(ys_ref.at[pl.ds(dest_ref[k, t], 1)], gbuf.at[k, pl.ds(t, 1)],
                                      sem).start(priority=k % 2)
        return carry

    lax.fori_loop(0, tm * TOP_K // GATHER_UNROLL, issue, 0)
    pp = _dot(p_ref[...].astype(BF16), wpp_ref[...])
    for k in range(TOP_K):
        pltpu.make_async_copy(ys_ref.at[pl.ds(0, tm)], gbuf.at[k], sem).wait()
    gates = gate_ref[...]
    x2 = x1_ref[...]
    for k in range(TOP_K):
        x2 = x2 + gates[:, k:k + 1] * gbuf[k]
    hn = (_rms(x2, D_MODEL) * gple_ref[...]).astype(BF16)
    o_ref[...] = x2 + jax.nn.sigmoid(_dot(hn, wpg_ref[...])) * pp


def _final(dest, x1, gates_t, p, ys, wts, tile_off, n_tok, tm=TM_TOK):
    in_specs = [
        pl.BlockSpec((TOP_K, tm), lambda i: (0, i + tile_off), memory_space=pltpu.SMEM),
        pl.BlockSpec((tm, D_MODEL), lambda i: (i + tile_off, 0)),
        pl.BlockSpec((tm, TOP_K), lambda i: (i + tile_off, 0)),
        pl.BlockSpec((tm, PLE_DIM), lambda i: (i, 0)),
        pl.BlockSpec(memory_space=pl.ANY),
    ] + [_full_spec(w.shape) for w in wts]
    return pl.pallas_call(
        functools.partial(_final_body, tm=tm),
        out_shape=jax.ShapeDtypeStruct((n_tok, D_MODEL), F32),
        grid=(n_tok // tm,), in_specs=in_specs,
        out_specs=pl.BlockSpec((tm, D_MODEL), lambda i: (i, 0)),
        scratch_shapes=[pltpu.VMEM((TOP_K, tm, D_MODEL), F32), pltpu.SemaphoreType.DMA(())],
        compiler_params=pltpu.CompilerParams(dimension_semantics=("arbitrary",),
                                             vmem_limit_bytes=VMEM_LIMIT),
        name="final",
    )(dest, x1, gates_t, p, ys, *wts)


def _rope_angles(pos):
    inv = ROPE_THETA ** (-jnp.arange(ROPE_HALF, dtype=F32) / ROPE_HALF)
    ang = pos.astype(F32)[:, None] * inv
    return jnp.cos(ang), jnp.sin(ang)


def _rope_row_tables(pos):
    cos, sin = _rope_angles(pos)
    n = pos.shape[0]
    z = lambda w: jnp.zeros((n, w), F32)
    c_tab = jnp.concatenate([jnp.ones((n, QK_NOPE), F32), cos, cos, z(LANES - QK_HEAD)], axis=1)
    sa_tab = jnp.concatenate([z(QK_NOPE), -sin, z(ROPE_HALF), z(LANES - QK_HEAD)], axis=1)
    sb_tab = jnp.concatenate([z(QK_NOPE), z(ROPE_HALF), sin, z(LANES - QK_HEAD)], axis=1)
    return c_tab, sa_tab, sb_tab


def _pad_lanes(v, width=LANES):
    return jnp.pad(v, [(0, 0)] * (v.ndim - 1) + [(0, width - v.shape[-1])])


def kernel(x_prompt, x_sample, cache_ckv, cache_kpe, state_pool, page_table, p_prompt, p_sample, g_mix, w_in, g_q_a, w_q_b, g_kv_a, w_kv_b, g_q_head, g_k_head, w_pool, pool_scale, w_out, g_ffn, w_router, b_router, w_gate, b_gate, w_up, b_up, w_down, b_down, g_ple, w_ple_gate, w_ple_proj):
    assert x_prompt.shape[-1] == D_MODEL and g_mix.shape[0] == 1
    b, s, _ = x_prompt.shape
    nb, n_new, _ = x_sample.shape
    n_p, n_s = b * s, nb * n_new
    past = page_table.shape[1] * cache_ckv.shape[2]
    row = lambda v: v.reshape(1, -1).astype(F32)

    wi = w_in[0]
    w_cq, w_ckv = wi[:, :Q_LORA], wi[:, Q_LORA:Q_LORA + KV_LORA]
    w_kpe = wi[:, Q_LORA + KV_LORA:Q_LORA + KV_LORA + ROPE_DIM]
    w_u = wi[:, Q_LORA + KV_LORA + ROPE_DIM:]
    w_kpe128 = jnp.pad(w_kpe, ((0, 0), (QK_NOPE, LANES - QK_HEAD)))
    w_main = jnp.concatenate([w_cq, w_ckv, w_u, w_kpe128], axis=1).astype(BF16)
    wq = _pad_lanes(w_q_b[0]).reshape(Q_LORA, N_HEADS * LANES).astype(BF16)
    wk_nope = w_kv_b[0][:, :, :QK_NOPE]
    wk = _pad_lanes(wk_nope).reshape(KV_LORA, N_HEADS * LANES).astype(BF16)
    wv = w_kv_b[0][:, :, QK_NOPE:].reshape(KV_LORA, ATTN_WIDTH).astype(BF16)
    gq128 = _pad_lanes(row(g_q_head[0]))
    gk128 = _pad_lanes(row(g_k_head[0]))
    gkn128 = _pad_lanes(row(g_k_head[0][:QK_NOPE]))
    wkt = jnp.transpose(wk_nope, (1, 2, 0))
    wkt128 = jnp.pad(wkt, ((0, 0), (0, LANES - QK_NOPE), (0, 0))).astype(BF16)
    wkt_flat = wkt.reshape(N_HEADS * QK_NOPE, KV_LORA).astype(BF16)
    gpe_b = jnp.broadcast_to(g_k_head[0][QK_NOPE:].astype(F32)[:, None], (ROPE_DIM, LANES))
    wpool = w_pool[0].astype(BF16)
    pscale = row(pool_scale[0])
    wout_a = w_out[0][:ATTN_WIDTH].astype(BF16)
    wout_p = w_out[0][ATTN_WIDTH:].astype(BF16)
    wr_t = w_router[0].T.astype(F32)
    wr_hi = wr_t.astype(BF16)
    wr_lo = (wr_t - wr_hi.astype(F32)).astype(BF16)
    br_b = jnp.broadcast_to(b_router[0].astype(F32)[:, None], (N_EXPERTS, LANES))

    tabs_p = _rope_row_tables(jnp.arange(s, dtype=jnp.int32))
    pos_s = past + (jnp.arange(TM_PROJ, dtype=jnp.int32) % n_new)
    tabs_s = _rope_row_tables(pos_s)
    cos_t, sin_t = (t.T for t in _rope_angles(jnp.arange(past + LANES, dtype=jnp.int32)))
    tabs_attn = (cos_t[:, :past], sin_t[:, :past], cos_t[:, past:], sin_t[:, past:])

    ckv_p, kpe_p, q_p, k_p, v_p, xmid_p, pst_p = _proj_prompt(
        x_prompt, tabs_p,
        (row(g_mix[0]), w_main, row(g_q_a[0]), row(g_kv_a[0]), wq, gq128, wk, wv, gk128, wpool, pscale, wout_p))
    attn_p = _flash_prompt(q_p, k_p, v_p, b, s)

    xs_flat = x_sample.reshape(n_s, D_MODEL)
    ckv_s, kpe_s, u_s, qf_s, qlat_s = _proj_sample(
        xs_flat, tabs_s, (row(g_mix[0]), w_main, row(g_q_a[0]), row(g_kv_a[0]), wq, gq128, wkt128, gkn128))
    u_s3 = u_s.reshape(nb, n_new, POOL_WIDTH)
    pool_t = _pool_sample(jnp.transpose(state_pool[0], (1, 0, 2)), jnp.transpose(u_s3, (1, 0, 2)), wpool, pscale)
    pool_s = jnp.transpose(pool_t, (1, 0, 2)).reshape(n_s, POOL_WIDTH)
    kpe_new_t = _pad_lanes(jnp.transpose(kpe_s.reshape(nb, n_new, ROPE_DIM), (0, 2, 1)))
    attn_s = _sample_attn(page_table, cache_ckv[0], jnp.transpose(cache_kpe[0], (0, 2, 1)), qlat_s, qf_s,
                          ckv_s.reshape(nb, n_new, KV_LORA), kpe_new_t,
                          tabs_attn, (wkt_flat, gpe_b, wv)).reshape(n_s, ATTN_WIDTH)

    x1, h2, topi, gates, rank, counts = _post_attn(
        xmid_p, attn_p, xs_flat, attn_s, pool_s, (wout_a, wout_p, row(g_ffn[0]), wr_hi, wr_lo, br_b))
    nt = n_p + n_s
    counts = counts[:, 0]
    padded = (counts + TB - 1) // TB * TB
    ends = jnp.cumsum(padded)
    starts = ends - padded
    eids = jnp.arange(N_EXPERTS, dtype=jnp.int32)
    dest = (jnp.sum(jnp.where(topi[..., None] == eids, starts, 0), axis=-1) + rank).astype(jnp.int32)
    n_rows = nt * TOP_K + N_EXPERTS * TB
    n_blocks = n_rows // TB
    n_used = (ends[-1] // TB).astype(jnp.int32).reshape(1)
    blk_start = jnp.arange(n_blocks, dtype=jnp.int32) * TB
    blk_start = jnp.minimum(blk_start, jnp.maximum(ends[-1] - TB, 0))
    block_expert = jnp.minimum(jnp.sum((ends[None, :] <= blk_start[:, None]).astype(jnp.int32), axis=1),
                               N_EXPERTS - 1).astype(jnp.int32)

    row_token = _invert(dest, n_rows)
    b3 = lambda v: v[0].reshape(N_EXPERTS, 1, D_MODEL).astype(F32)
    ys = _experts(block_expert, n_used, row_token, h2, w_gate[0], b3(b_gate), w_up[0], b3(b_up), w_down[0], b3(b_down))
    gates_t = gates.T
    fin_w = (row(g_ple[0]), w_ple_gate[0].astype(BF16), w_ple_proj[0].astype(BF16))
    y_p = _final(dest, x1, gates_t, p_prompt[0].reshape(n_p, PLE_DIM), ys, fin_w, 0, n_p)
    y_s = _final(dest, x1, gates_t, p_sample[0].reshape(n_s, PLE_DIM), ys, fin_w, n_p // TM_TOK, n_s)

    pool_state_s = jnp.concatenate([state_pool[0][:, n_new:], u_s3], axis=1)
    return (y_p.reshape(b, s, D_MODEL), y_s.reshape(nb, n_new, D_MODEL),
            ckv_p.reshape(1, b, s, KV_LORA), kpe_p.reshape(1, b, s, ROPE_DIM), pst_p[:, HALO - POOL_PREV:][None],
            ckv_s.reshape(1, nb, n_new, KV_LORA), kpe_s.reshape(1, nb, n_new, ROPE_DIM), pool_state_s[None])
```

```python
import functools

import jax
import jax.numpy as jnp
from jax import lax
from jax.experimental import pallas as pl
from jax.experimental.pallas import tpu as pltpu

F32 = jnp.float32
BF16 = jnp.bfloat16

D_MODEL = 1024
N_HEADS = 8
QK_NOPE = 64
ROPE_DIM = 32
ROPE_HALF = ROPE_DIM // 2
QK_HEAD = QK_NOPE + ROPE_DIM
V_HEAD = 64
Q_LORA = 384
KV_LORA = 256
ATTN_WIDTH = N_HEADS * V_HEAD
ROPE_THETA = 10000.0
SM_SCALE = QK_HEAD ** -0.5
POOL_WINDOWS = (2, 4, 8, 16)
POOL_GROUP_DIM = 128
POOL_WIDTH = 512
POOL_PREV = 15
N_EXPERTS = 32
TOP_K = 4
SWIGLU_LIMIT = 7.0
SWIGLU_ALPHA = 1.702
PLE_DIM = 256
EPS = 1e-6

LANES = 128
HALO = 16
MAIN_COLS = Q_LORA + KV_LORA + POOL_WIDTH + LANES
VMEM_LIMIT = 56 * 1024 * 1024

TM_PROJ = 256
TQ, TK = 256, 1024
PAGES_PER_STEP = 16
RING = 3
SUB = 256
TM_TOK = 256
TM_DISP = 512
TB = 256
GATHER_UNROLL = 8

NEG_INF = float("-inf")


def _dot(a, b):
    return jnp.dot(a, b, preferred_element_type=F32)


def _dot_nt(a, b):
    return lax.dot_general(a, b, (((1,), (1,)), ((), ())), preferred_element_type=F32)


def _rms(x, n):
    return x * lax.rsqrt(jnp.sum(x * x, axis=-1, keepdims=True) * (1.0 / n) + EPS)


def _rope_rows(x, c_tab, sa_tab, sb_tab):
    return (x * c_tab + pltpu.roll(x, LANES - ROPE_HALF, 1) * sa_tab
            + pltpu.roll(x, ROPE_HALF, 1) * sb_tab)


def _proj_body(*refs, tm, seq_mode):
    if seq_mode:
        (x_ref, c_ref, sa_ref, sb_ref, gmix_ref, wmain_ref, gqa_ref, gkva_ref, wq_ref, gq_ref,
         wk_ref, wv_ref, gk_ref, wpool_ref, pscale_ref, woutp_ref,
         ckv_ref, kpe_ref, q_ref, k_ref, v_ref, xmid_ref, pst_ref, ubuf) = refs
    else:
        (x_ref, c_ref, sa_ref, sb_ref, gmix_ref, wmain_ref, gqa_ref, gkva_ref, wq_ref, gq_ref,
         wkt_ref, gkn_ref,
         ckv_ref, kpe_ref, u_ref, q_ref, qlat_ref) = refs

    x = x_ref[...]
    h = (_rms(x, D_MODEL) * gmix_ref[...]).astype(BF16)
    z = _dot(h, wmain_ref[...])
    cqn = (_rms(z[:, :Q_LORA], Q_LORA) * gqa_ref[...]).astype(BF16)
    ckvn = _rms(z[:, Q_LORA:Q_LORA + KV_LORA], KV_LORA) * gkva_ref[...]
    u = z[:, Q_LORA + KV_LORA:Q_LORA + KV_LORA + POOL_WIDTH]
    kpe128 = z[:, Q_LORA + KV_LORA + POOL_WIDTH:]
    ckv_ref[...] = ckvn
    kpe_ref[...] = pltpu.roll(kpe128, LANES - QK_NOPE, 1)[:, :ROPE_DIM]

    c_tab, sa_tab, sb_tab = c_ref[...], sa_ref[...], sb_ref[...]
    q_all = _dot(cqn, wq_ref[...])
    gq = gq_ref[...]
    for hd in range(N_HEADS):
        qh = q_all[:, hd * LANES:(hd + 1) * LANES]
        qh = _rope_rows(_rms(qh, QK_HEAD) * gq, c_tab, sa_tab, sb_tab) * SM_SCALE
        if seq_mode:
            q_ref[hd] = qh.astype(BF16)
        else:
            q_ref[hd] = qh
            qlat_ref[hd] = _dot((qh * gkn_ref[...]).astype(BF16), wkt_ref[hd])

    if not seq_mode:
        u_ref[...] = u
        return

    ckvb = ckvn.astype(BF16)
    k_all = _dot(ckvb, wk_ref[...])
    gk = gk_ref[...]
    for hd in range(N_HEADS):
        kh = k_all[:, hd * LANES:(hd + 1) * LANES] + kpe128
        kh = _rope_rows(_rms(kh, QK_HEAD) * gk, c_tab, sa_tab, sb_tab)
        k_ref[hd] = kh.astype(BF16)
    v_all = _dot(ckvb, wv_ref[...])
    for hp in range(N_HEADS // 2):
        v_ref[hp] = v_all[:, hp * LANES:(hp + 1) * LANES].astype(BF16)

    j = pl.program_id(1)

    @pl.when(j == 0)
    def _():
        ubuf[0:HALO, :] = jnp.zeros((HALO, POOL_WIDTH), F32)

    ubuf[HALO:HALO + tm, :] = u
    pos1 = (j * tm + 1 + lax.broadcasted_iota(jnp.int32, (tm, 1), 0)).astype(F32)
    pool_parts = []
    for g, w in enumerate(POOL_WINDOWS):
        sl = slice(g * POOL_GROUP_DIM, (g + 1) * POOL_GROUP_DIM)
        ug = u[:, sl]
        acc = ug
        for back in range(1, w):
            acc = acc + ubuf[HALO - back:HALO - back + tm, sl]
        d = acc / jnp.minimum(float(w), pos1) - ug
        pool_parts.append(_dot(d.astype(BF16), wpool_ref[g]) * pscale_ref[:, sl])
    pool_out = jnp.concatenate(pool_parts, axis=1).astype(BF16)
    xmid_ref[...] = x + _dot(pool_out, woutp_ref[...])

    tail = ubuf[tm:tm + HALO, :]
    ubuf[0:HALO, :] = tail

    @pl.when(j == pl.num_programs(1) - 1)
    def _():
        pst_ref[...] = tail


def _full_spec(shape):
    nd = len(shape)
    return pl.BlockSpec(shape, lambda *_: (0,) * nd)


def _proj_prompt(x, tabs, wts, tm=TM_PROJ):
    b, s, _ = x.shape
    n = b * s
    nj = s // tm
    c_tab, sa_tab, sb_tab = tabs
    row = lambda width: pl.BlockSpec((tm, width), lambda bi, j: (bi * nj + j, 0))
    tab = pl.BlockSpec((tm, LANES), lambda bi, j: (j, 0))
    heads = lambda nh, width: pl.BlockSpec((nh, tm, width), lambda bi, j: (0, bi * nj + j, 0))
    in_specs = [pl.BlockSpec((None, tm, D_MODEL), lambda bi, j: (bi, j, 0)), tab, tab, tab] + [
        _full_spec(w.shape) for w in wts]
    out_shape = (
        jax.ShapeDtypeStruct((n, KV_LORA), F32),
        jax.ShapeDtypeStruct((n, ROPE_DIM), F32),
        jax.ShapeDtypeStruct((N_HEADS, n, LANES), BF16),
        jax.ShapeDtypeStruct((N_HEADS, n, LANES), BF16),
        jax.ShapeDtypeStruct((N_HEADS // 2, n, LANES), BF16),
        jax.ShapeDtypeStruct((n, D_MODEL), F32),
        jax.ShapeDtypeStruct((b, HALO, POOL_WIDTH), F32),
    )
    out_specs = (row(KV_LORA), row(ROPE_DIM), heads(N_HEADS, LANES), heads(N_HEADS, LANES),
                 heads(N_HEADS // 2, LANES), row(D_MODEL),
                 pl.BlockSpec((None, HALO, POOL_WIDTH), lambda bi, j: (bi, 0, 0)))
    return pl.pallas_call(
        functools.partial(_proj_body, tm=tm, seq_mode=True),
        out_shape=out_shape, grid=(b, nj), in_specs=in_specs, out_specs=out_specs,
        scratch_shapes=[pltpu.VMEM((tm + HALO, POOL_WIDTH), F32)],
        compiler_params=pltpu.CompilerParams(dimension_semantics=("arbitrary", "arbitrary"),
                                             vmem_limit_bytes=VMEM_LIMIT),
        name="proj_prompt",
    )(x, c_tab, sa_tab, sb_tab, *wts)


def _proj_sample(x, tabs, wts, tm=TM_PROJ):
    n = x.shape[0]
    c_tab, sa_tab, sb_tab = tabs
    row = lambda width: pl.BlockSpec((tm, width), lambda i: (i, 0))
    heads = lambda width: pl.BlockSpec((N_HEADS, tm, width), lambda i: (0, i, 0))
    in_specs = [row(D_MODEL), _full_spec((tm, LANES)), _full_spec((tm, LANES)), _full_spec((tm, LANES))] + [
        _full_spec(w.shape) for w in wts]
    out_shape = (
        jax.ShapeDtypeStruct((n, KV_LORA), F32),
        jax.ShapeDtypeStruct((n, ROPE_DIM), F32),
        jax.ShapeDtypeStruct((n, POOL_WIDTH), F32),
        jax.ShapeDtypeStruct((N_HEADS, n, LANES), F32),
        jax.ShapeDtypeStruct((N_HEADS, n, KV_LORA), F32),
    )
    out_specs = (row(KV_LORA), row(ROPE_DIM), row(POOL_WIDTH), heads(LANES), heads(KV_LORA))
    return pl.pallas_call(
        functools.partial(_proj_body, tm=tm, seq_mode=False),
        out_shape=out_shape, grid=(n // tm,), in_specs=in_specs, out_specs=out_specs,
        compiler_params=pltpu.CompilerParams(dimension_semantics=("arbitrary",),
                                             vmem_limit_bytes=VMEM_LIMIT),
        name="proj_sample",
    )(x, c_tab, sa_tab, sb_tab, *wts)


def _flash_body(q_ref, k_ref, v_ref, o_ref, m_sc, l_sc, acc_sc, *, tq, tk):
    qi = pl.program_id(2)
    n_full = (qi * tq) // tk
    row = qi * tq + lax.broadcasted_iota(jnp.int32, (tq, tk), 0)
    col = lax.broadcasted_iota(jnp.int32, (tq, tk), 1)
    m_sc[...] = jnp.full(m_sc.shape, NEG_INF, F32)
    l_sc[...] = jnp.zeros(l_sc.shape, F32)
    acc_sc[...] = jnp.zeros(acc_sc.shape, F32)

    def step(ki, masked):
        start = pl.multiple_of(ki * tk, tk)
        v = v_ref[pl.ds(start, tk), :]
        for hh in range(2):
            s = _dot_nt(q_ref[hh], k_ref[hh, pl.ds(start, tk), :])
            if masked:
                s = jnp.where(col + start <= row, s, NEG_INF)
            m_prev = m_sc[hh]
            m_new = jnp.maximum(m_prev, jnp.max(s, axis=-1, keepdims=True))
            alpha = jnp.exp(m_prev - m_new)
            p = jnp.exp(s - m_new)
            l_sc[hh] = alpha * l_sc[hh] + jnp.sum(p, axis=-1, keepdims=True)
            acc_sc[hh] = alpha * acc_sc[hh] + _dot(p.astype(BF16), v)
            m_sc[hh] = m_new

    def loop_body(ki, carry):
        step(ki, False)
        return carry

    lax.fori_loop(0, n_full, loop_body, 0)
    step(n_full, True)
    lane = lax.broadcasted_iota(jnp.int32, (tq, LANES), 1)
    o_ref[...] = jnp.where(lane < V_HEAD, acc_sc[0] / l_sc[0], acc_sc[1] / l_sc[1]).astype(o_ref.dtype)


def _flash_prompt(q, k, v, b, s, tq=TQ, tk=TK):
    n = b * s
    nq = s // tq
    return pl.pallas_call(
        functools.partial(_flash_body, tq=tq, tk=tk),
        out_shape=jax.ShapeDtypeStruct((n, ATTN_WIDTH), BF16),
        grid=(b, N_HEADS // 2, nq),
        in_specs=[
            pl.BlockSpec((2, tq, LANES), lambda bi, hp, qi: (hp, bi * nq + qi, 0)),
            pl.BlockSpec((2, s, LANES), lambda bi, hp, qi: (hp, bi, 0)),
            pl.BlockSpec((None, s, LANES), lambda bi, hp, qi: (hp, bi, 0)),
        ],
        out_specs=pl.BlockSpec((tq, LANES), lambda bi, hp, qi: (bi * nq + qi, hp)),
        scratch_shapes=[pltpu.VMEM((2, tq, 1), F32), pltpu.VMEM((2, tq, 1), F32), pltpu.VMEM((2, tq, LANES), F32)],
        compiler_params=pltpu.CompilerParams(dimension_semantics=("arbitrary",) * 3,
                                             vmem_limit_bytes=VMEM_LIMIT),
        name="flash_prompt",
    )(q, k, v)


def _pool_sample_body(st_ref, u_ref, wpool_ref, pscale_ref, o_ref, *, n_new):
    for t in range(n_new):
        for g, w in enumerate(POOL_WINDOWS):
            sl = slice(g * POOL_GROUP_DIM, (g + 1) * POOL_GROUP_DIM)
            ug = u_ref[t, :, sl]
            acc = ug
            for back in range(1, w):
                src = t - back
                acc = acc + (u_ref[src, :, sl] if src >= 0 else st_ref[POOL_PREV + src, :, sl])
            d = acc / float(w) - ug
            o_ref[t, :, sl] = _dot(d.astype(BF16), wpool_ref[g]) * pscale_ref[:, sl]


def _pool_sample(state_t, u_t, wpool, pscale):
    n_new, nb, _ = u_t.shape
    return pl.pallas_call(
        functools.partial(_pool_sample_body, n_new=n_new),
        out_shape=jax.ShapeDtypeStruct((n_new, nb, POOL_WIDTH), F32),
        grid=(1,),
        in_specs=[_full_spec(state_t.shape), _full_spec(u_t.shape), _full_spec(wpool.shape),
                  _full_spec(pscale.shape)],
        out_specs=_full_spec((n_new, nb, POOL_WIDTH)),
        compiler_params=pltpu.CompilerParams(dimension_semantics=("arbitrary",),
                                             vmem_limit_bytes=VMEM_LIMIT),
        name="pool_sample",
    )(state_t, u_t, wpool, pscale)


def _sample_attn_body(pt_ref, ckv_hbm, kpe_hbm, qlat_ref, q_ref, ckvn_ref, kpen_ref, cos_ref, sin_ref,
                      cost_ref, sint_ref, wkt_ref, gpe_ref, wv_ref, o_ref,
                      cbuf, kbuf, sem_c, sem_k, a_sc, m_sc, l_sc, acc_sc, cbb0, cbb1, sb0, sb1,
                      *, n_pages, n_new):
    npg = PAGES_PER_STEP
    page = LANES
    n_chunks = n_pages // npg
    b = pl.program_id(0)
    nb = pl.num_programs(0)
    rows = N_HEADS * n_new

    def page_copies(seq, chunk, slot):
        cps = []
        for i in range(npg):
            pg = pt_ref[seq, chunk * npg + i]
            cps.append(pltpu.make_async_copy(ckv_hbm.at[pg], cbuf.at[slot, pl.ds(i * page, page)], sem_c.at[slot]))
            cps.append(pltpu.make_async_copy(kpe_hbm.at[pg], kbuf.at[slot, i], sem_k.at[slot]))
        return cps

    def start_chunk(seq, chunk, slot):
        for cp in page_copies(seq, chunk, slot):
            cp.start()

    def wait_chunk(slot):
        for cp in page_copies(0, 0, slot):
            cp.wait()

    total = nb * n_chunks

    @pl.when(b == 0)
    def _():
        for g0 in range(RING - 1):
            start_chunk(g0 // n_chunks, g0 % n_chunks, g0 % RING)

    a_sc[0:N_HEADS * QK_NOPE, :] = wkt_ref[...]
    a_sc[N_HEADS * QK_NOPE:, :] = qlat_ref[...].reshape(rows, KV_LORA).astype(BF16)
    m_sc[...] = jnp.full((rows, LANES), NEG_INF, F32)
    l_sc[...] = jnp.zeros((rows, LANES), F32)
    acc_sc[...] = jnp.zeros((rows, KV_LORA), F32)
    qpe = q_ref[...].reshape(rows, LANES)[:, QK_NOPE:QK_HEAD].astype(BF16)
    g1 = gpe_ref[0:ROPE_HALF, :]
    g2 = gpe_ref[ROPE_HALF:ROPE_DIM, :]

    def scores(cb, kpt, cos, sin):
        ck = cb.shape[0]
        r = _dot_nt(a_sc[...], cb)
        kt = r[:N_HEADS * QK_NOPE]
        ssn = jnp.sum((kt * kt).reshape(N_HEADS, QK_NOPE, ck), axis=1)
        x1 = kpt[0:ROPE_HALF]
        x2 = kpt[ROPE_HALF:ROPE_DIM]
        ssp = jnp.sum(x1 * x1 + x2 * x2, axis=0, keepdims=True)
        rs = lax.rsqrt((ssn + ssp) * (1.0 / QK_HEAD) + EPS)
        reps = ck // LANES
        x1 = x1 * jnp.tile(g1, (1, reps))
        x2 = x2 * jnp.tile(g2, (1, reps))
        rot = jnp.concatenate([x1 * cos - x2 * sin, x2 * cos + x1 * sin], axis=0).astype(BF16)
        s = r[N_HEADS * QK_NOPE:] + _dot(qpe, rot)
        s = s.reshape(N_HEADS, n_new, ck) * rs[:, None, :]
        return s.reshape(rows, ck)

    def update(s_all, cbs):
        m_prev = m_sc[:, :1]
        m_new = jnp.maximum(m_prev, jnp.max(s_all, axis=-1, keepdims=True))
        alpha = jnp.exp(m_prev - m_new)
        p = jnp.exp(s_all - m_new)
        l_sc[...] = jnp.broadcast_to(alpha * l_sc[:, :1] + jnp.sum(p, axis=-1, keepdims=True),
                                     (rows, LANES))
        acc = alpha * acc_sc[...]
        off = 0
        for cb in cbs:
            ck = cb.shape[0]
            acc = acc + _dot(p[:, off:off + ck].astype(BF16), cb)
            off += ck
        acc_sc[...] = acc
        m_sc[...] = jnp.broadcast_to(m_new, (rows, LANES))

    pages_per_sub = SUB // page

    n_sub = npg // pages_per_sub

    def scores_chunk(jc, cbb, s_buf):
        g = b * n_chunks + jc
        slot = g % RING
        wait_chunk(slot)
        nxt = jnp.minimum(g + RING - 1, total - 1)
        start_chunk(nxt // n_chunks, nxt % n_chunks, (g + RING - 1) % RING)
        for sc in range(n_sub):
            cb = cbuf[slot, sc * SUB:(sc + 1) * SUB, :].astype(BF16)
            cbb[sc * SUB:(sc + 1) * SUB, :] = cb
            kpt = jnp.concatenate([kbuf[slot, sc * pages_per_sub + i] for i in range(pages_per_sub)], axis=1)
            start = pl.multiple_of(jc * (npg * page) + sc * SUB, SUB)
            s_buf[:, sc * SUB:(sc + 1) * SUB] = scores(
                cb, kpt, cos_ref[:, pl.ds(start, SUB)], sin_ref[:, pl.ds(start, SUB)])

    def update_chunk(cbb, s_buf):
        update(s_buf[...], [cbb[sc * SUB:(sc + 1) * SUB, :] for sc in range(n_sub)])

    scores_chunk(0, cbb0, sb0)

    pad = LANES - n_new
    cb_new = jnp.concatenate([ckvn_ref[...], jnp.zeros((pad, KV_LORA), F32)], axis=0).astype(BF16)
    s_new = scores(cb_new, kpen_ref[...], cost_ref[...], sint_ref[...])
    key = lax.broadcasted_iota(jnp.int32, (rows, LANES), 1)
    tok = lax.broadcasted_iota(jnp.int32, (rows, LANES), 0) % n_new
    update(jnp.where(key <= tok, s_new, NEG_INF), [cb_new])

    def pair(i, carry):
        scores_chunk(2 * i + 1, cbb1, sb1)
        update_chunk(cbb0, sb0)
        scores_chunk(2 * i + 2, cbb0, sb0)
        update_chunk(cbb1, sb1)
        return carry

    lax.fori_loop(0, n_chunks // 2 - 1, pair, 0)
    scores_chunk(n_chunks - 1, cbb1, sb1)
    update_chunk(cbb0, sb0)
    update_chunk(cbb1, sb1)

    @pl.when(b == nb - 1)
    def _():
        for extra in range(RING - 1):
            wait_chunk((total + extra) % RING)

    o_lat = (acc_sc[...] / l_sc[:, :1]).astype(BF16)
    full = _dot(o_lat, wv_ref[...]).reshape(N_HEADS, n_new, ATTN_WIDTH)
    hd = lax.broadcasted_iota(jnp.int32, (N_HEADS, n_new, ATTN_WIDTH), 0)
    colh = lax.broadcasted_iota(jnp.int32, (N_HEADS, n_new, ATTN_WIDTH), 2) // V_HEAD
    o_ref[...] = jnp.sum(jnp.where(hd == colh, full, 0.0), axis=0)


def _sample_attn(page_table, cache_ckv, cache_kpe_t, qlat, qf, ckv_new, kpe_new_t, tabs, wts):
    nb, n_pages = page_table.shape
    page = cache_ckv.shape[1]
    n_new = ckv_new.shape[1]
    npg = PAGES_PER_STEP
    assert page == LANES and n_pages % (2 * npg) == 0 and nb * (n_pages // npg) >= RING
    cos_t, sin_t, cos_tail, sin_tail = tabs
    rows = N_HEADS * n_new
    const = lambda shape: pl.BlockSpec(shape, lambda b, pt, nd=len(shape): (0,) * nd)
    in_specs = [
        pl.BlockSpec(memory_space=pl.ANY), pl.BlockSpec(memory_space=pl.ANY),
        pl.BlockSpec((N_HEADS, n_new, KV_LORA), lambda b, pt: (0, b, 0)),
        pl.BlockSpec((N_HEADS, n_new, LANES), lambda b, pt: (0, b, 0)),
        pl.BlockSpec((None, n_new, KV_LORA), lambda b, pt: (b, 0, 0)),
        pl.BlockSpec((None, ROPE_DIM, LANES), lambda b, pt: (b, 0, 0)),
        const(cos_t.shape), const(sin_t.shape), const(cos_tail.shape), const(sin_tail.shape),
    ] + [const(w.shape) for w in wts]
    grid_spec = pltpu.PrefetchScalarGridSpec(
        num_scalar_prefetch=1, grid=(nb,), in_specs=in_specs,
        out_specs=pl.BlockSpec((None, n_new, ATTN_WIDTH), lambda b, pt: (b, 0, 0)),
        scratch_shapes=[pltpu.VMEM((RING, npg * page, KV_LORA), F32),
                        pltpu.VMEM((RING, npg, ROPE_DIM, page), F32),
                        pltpu.SemaphoreType.DMA((RING,)), pltpu.SemaphoreType.DMA((RING,)),
                        pltpu.VMEM((N_HEADS * QK_NOPE + rows, KV_LORA), BF16),
                        pltpu.VMEM((rows, LANES), F32), pltpu.VMEM((rows, LANES), F32),
                        pltpu.VMEM((rows, KV_LORA), F32),
                        pltpu.VMEM((npg * page, KV_LORA), BF16), pltpu.VMEM((npg * page, KV_LORA), BF16),
                        pltpu.VMEM((rows, npg * page), F32), pltpu.VMEM((rows, npg * page), F32)])
    return pl.pallas_call(
        functools.partial(_sample_attn_body, n_pages=n_pages, n_new=n_new),
        out_shape=jax.ShapeDtypeStruct((nb, n_new, ATTN_WIDTH), F32),
        grid_spec=grid_spec,
        compiler_params=pltpu.CompilerParams(dimension_semantics=("arbitrary",),
                                             vmem_limit_bytes=VMEM_LIMIT),
        name="sample_attn",
    )(page_table, cache_ckv, cache_kpe_t, qlat, qf, ckv_new, kpe_new_t, cos_t, sin_t, cos_tail, sin_tail, *wts)


def _post_attn_body(xmid_ref, attn_p_ref, xs_ref, attn_s_ref, pool_s_ref, wouta_ref, woutp_ref, gffn_ref,
                    wrh_ref, wrl_ref, br_ref,
                    x1_ref, h2_ref, topi_ref, gate_ref, rank_ref, cnt_ref, base_sc, xm_sc, at_sc,
                    *, tm, n_prompt_tiles):
    i = pl.program_id(0)

    @pl.when(i == 0)
    def _():
        base_sc[...] = jnp.zeros((N_EXPERTS, LANES), F32)

    @pl.when(i < n_prompt_tiles)
    def _():
        xm_sc[...] = xmid_ref[...]
        at_sc[...] = attn_p_ref[...]

    @pl.when(i >= n_prompt_tiles)
    def _():
        xm_sc[...] = xs_ref[...] + _dot(pool_s_ref[...].astype(BF16), woutp_ref[...])
        at_sc[...] = attn_s_ref[...].astype(BF16)

    x1 = xm_sc[...] + _dot(at_sc[...], wouta_ref[...])
    x1_ref[...] = x1
    h2 = _rms(x1, D_MODEL) * gffn_ref[...]
    h2_ref[...] = h2
    h_hi = h2.astype(BF16)
    h_lo = (h2 - h_hi.astype(F32)).astype(BF16)
    lg = (_dot_nt(wrh_ref[...], h_hi) + _dot_nt(wrh_ref[...], h_lo) + _dot_nt(wrl_ref[...], h_hi)
          + br_ref[:, :1])
    eid = lax.broadcasted_iota(jnp.int32, (N_EXPERTS, tm), 0)
    vals, idxs, hots = [], [], []
    for _ in range(TOP_K):
        mx = jnp.max(lg, axis=0, keepdims=True)
        ix = jnp.min(jnp.where(lg == mx, eid, N_EXPERTS), axis=0, keepdims=True)
        hot = eid == ix
        lg = jnp.where(hot, NEG_INF, lg)
        vals.append(mx)
        idxs.append(ix)
        hots.append(hot)
    ex = [jnp.exp(v - vals[0]) for v in vals]
    den = ex[0] + ex[1] + ex[2] + ex[3]
    gate_ref[...] = jnp.concatenate([e / den for e in ex], axis=0)
    topi_ref[...] = jnp.concatenate(idxs, axis=0)

    anyhot = (hots[0] | hots[1] | hots[2] | hots[3])
    hot_f = jnp.where(anyhot, 1.0, 0.0)
    upper = jnp.where(lax.broadcasted_iota(jnp.int32, (tm, tm), 0) < lax.broadcasted_iota(jnp.int32, (tm, tm), 1),
                      1.0, 0.0).astype(BF16)
    before = _dot(hot_f.astype(BF16), upper) + base_sc[:, :1]
    ranks = [jnp.sum(jnp.where(h, before, 0.0), axis=0, keepdims=True) for h in hots]
    rank_ref[...] = jnp.concatenate(ranks, axis=0).astype(jnp.int32)
    base_sc[...] = base_sc[...] + jnp.sum(hot_f, axis=1, keepdims=True)
    cnt_ref[...] = base_sc[...].astype(jnp.int32)


def _post_attn(xmid_p, attn_p, x_s, attn_s, pool_s, wts, tm=TM_TOK):
    n_p, n_s = xmid_p.shape[0], x_s.shape[0]
    npt, nst = n_p // tm, n_s // tm
    nt = n_p + n_s
    p_idx = lambda i: (jnp.minimum(i, npt - 1), 0)
    s_idx = lambda i: (jnp.maximum(i - npt, 0), 0)
    in_specs = [
        pl.BlockSpec((tm, D_MODEL), p_idx), pl.BlockSpec((tm, ATTN_WIDTH), p_idx),
        pl.BlockSpec((tm, D_MODEL), s_idx), pl.BlockSpec((tm, ATTN_WIDTH), s_idx),
        pl.BlockSpec((tm, POOL_WIDTH), s_idx),
    ] + [_full_spec(w.shape) for w in wts]
    col = lambda rows: pl.BlockSpec((rows, tm), lambda i: (0, i))
    out_shape = (
        jax.ShapeDtypeStruct((nt, D_MODEL), F32), jax.ShapeDtypeStruct((nt, D_MODEL), F32),
        jax.ShapeDtypeStruct((TOP_K, nt), jnp.int32), jax.ShapeDtypeStruct((TOP_K, nt), F32),
        jax.ShapeDtypeStruct((TOP_K, nt), jnp.int32), jax.ShapeDtypeStruct((N_EXPERTS, LANES), jnp.int32),
    )
    out_specs = (pl.BlockSpec((tm, D_MODEL), lambda i: (i, 0)), pl.BlockSpec((tm, D_MODEL), lambda i: (i, 0)),
                 col(TOP_K), col(TOP_K), col(TOP_K), _full_spec((N_EXPERTS, LANES)))
    return pl.pallas_call(
        functools.partial(_post_attn_body, tm=tm, n_prompt_tiles=npt),
        out_shape=out_shape, grid=(npt + nst,), in_specs=in_specs, out_specs=out_specs,
        scratch_shapes=[pltpu.VMEM((N_EXPERTS, LANES), F32), pltpu.VMEM((tm, D_MODEL), F32),
                        pltpu.VMEM((tm, ATTN_WIDTH), BF16)],
        compiler_params=pltpu.CompilerParams(dimension_semantics=("arbitrary",),
                                             vmem_limit_bytes=VMEM_LIMIT),
        name="post_attn",
    )(xmid_p, attn_p, x_s, attn_s, pool_s, *wts)


def _invert_body(dest_ref, zeros_hbm, rt_ref, sem, *, tm):
    i = pl.program_id(0)

    @pl.when(i == 0)
    def _():
        cp = pltpu.make_async_copy(zeros_hbm, rt_ref, sem)
        cp.start()
        cp.wait()

    base = i * tm

    def body(i, carry):
        for j in range(GATHER_UNROLL // TOP_K):
            t = i * (GATHER_UNROLL // TOP_K) + j
            for k in range(TOP_K):
                rt_ref[dest_ref[k, t]] = base + t
        return carry

    lax.fori_loop(0, tm * TOP_K // GATHER_UNROLL, body, 0)


def _invert(dest, n_rows, tm=TM_DISP):
    nt = dest.shape[1]
    return pl.pallas_call(
        functools.partial(_invert_body, tm=tm),
        out_shape=jax.ShapeDtypeStruct((n_rows,), jnp.int32),
        grid=(nt // tm,),
        in_specs=[pl.BlockSpec((TOP_K, tm), lambda i: (0, i), memory_space=pltpu.SMEM),
                  pl.BlockSpec(memory_space=pl.ANY)],
        out_specs=pl.BlockSpec(memory_space=pltpu.SMEM),
        scratch_shapes=[pltpu.SemaphoreType.DMA(())],
        compiler_params=pltpu.CompilerParams(dimension_semantics=("arbitrary",)),
        name="invert",
    )(dest, jnp.zeros((n_rows,), jnp.int32))


def _experts_body(be_ref, nu_ref, rt_cur_ref, rt_nxt_ref, h2_ref, wg_ref, bg_ref, wu_ref, bu_ref, wd_ref, bd_ref,
                  y_ref, xbuf, sem, wgb, wub, wdb, *, tb):
    blk = pl.program_id(0)
    slot = blk % 2
    n_used = nu_ref[0]

    @pl.when((blk == 0) | (be_ref[blk] != be_ref[jnp.maximum(blk - 1, 0)]))
    def _():
        wgb[...] = wg_ref[...].astype(BF16)
        wub[...] = wu_ref[...].astype(BF16)
        wdb[...] = wd_ref[...].astype(BF16)

    def gather(rt_ref, s):
        for r in range(tb):
            pltpu.make_async_copy(h2_ref.at[pl.ds(rt_ref[r], 1)], xbuf.at[s, pl.ds(r, 1)],
                                  sem.at[s]).start(priority=r % 2)

    @pl.when((blk == 0) & (n_used > 0))
    def _():
        gather(rt_cur_ref, 0)

    @pl.when(blk + 1 < n_used)
    def _():
        gather(rt_nxt_ref, 1 - slot)

    @pl.when(blk < n_used)
    def _():
        pltpu.make_async_copy(h2_ref.at[pl.ds(0, tb)], xbuf.at[slot], sem.at[slot]).wait()
        xb = xbuf[slot].astype(BF16)
        g = jnp.minimum(_dot(xb, wgb[...]) + bg_ref[...], SWIGLU_LIMIT)
        u = jnp.clip(_dot(xb, wub[...]) + bu_ref[...], -SWIGLU_LIMIT, SWIGLU_LIMIT)
        act = (u + 1.0) * (g * jax.nn.sigmoid(SWIGLU_ALPHA * g))
        y_ref[...] = _dot(act.astype(BF16), wdb[...]) + bd_ref[...]

    @pl.when(blk >= n_used)
    def _():
        y_ref[...] = jnp.zeros(y_ref.shape, F32)


def _experts(block_expert, n_used, row_token, h2, wg, bg, wu, bu, wd, bd, tb=TB):
    r = row_token.shape[0]
    nblk = r // tb
    w_spec = pl.BlockSpec((None, D_MODEL, D_MODEL), lambda b, be, nu: (be[b], 0, 0))
    b_spec = pl.BlockSpec((None, 1, D_MODEL), lambda b, be, nu: (be[b], 0, 0))
    rt_cur = pl.BlockSpec((tb,), lambda b, be, nu: (b,), memory_space=pltpu.SMEM)
    rt_nxt = pl.BlockSpec((tb,), lambda b, be, nu: (jnp.minimum(b + 1, nblk - 1),), memory_space=pltpu.SMEM)
    grid_spec = pltpu.PrefetchScalarGridSpec(
        num_scalar_prefetch=2, grid=(nblk,),
        in_specs=[rt_cur, rt_nxt, pl.BlockSpec(memory_space=pl.ANY),
                  w_spec, b_spec, w_spec, b_spec, w_spec, b_spec],
        out_specs=pl.BlockSpec((tb, D_MODEL), lambda b, be, nu: (b, 0)),
        scratch_shapes=[pltpu.VMEM((2, tb, D_MODEL), F32), pltpu.SemaphoreType.DMA((2,))]
        + [pltpu.VMEM((D_MODEL, D_MODEL), BF16)] * 3)
    return pl.pallas_call(
        functools.partial(_experts_body, tb=tb),
        out_shape=jax.ShapeDtypeStruct((r, D_MODEL), F32), grid_spec=grid_spec,
        compiler_params=pltpu.CompilerParams(dimension_semantics=("arbitrary",),
                                             vmem_limit_bytes=VMEM_LIMIT),
        name="experts",
    )(block_expert, n_used, row_token, row_token, h2, wg, bg, wu, bu, wd, bd)


def _final_body(dest_ref, x1_ref, gate_ref, p_ref, ys_ref, gple_ref, wpg_ref, wpp_ref, o_ref, gbuf, sem, *, tm):
    for t in range(tm):
        for k in range(TOP_K):
            pltpu.make_async_copy(ys_ref.at[pl.ds(dest_ref[k, t], 1)], gbuf.at[k, pl.ds(t, 1)],
                                  sem).start(priority=k % 2)
    pp = _dot(p_ref[...].astype(BF16), wpp_ref[...])
    for k in range(TOP_K):
        pltpu.make_async_copy(ys_ref.at[pl.ds(0, tm)], gbuf.at[k], sem).wait()
    gates = gate_ref[...]
    x2 = x1_ref[...]
    for k in range(TOP_K):
        x2 = x2 + gates[:, k:k + 1] * gbuf[k]
    hn = (_rms(x2, D_MODEL) * gple_ref[...]).astype(BF16)
    o_ref[...] = x2 + jax.nn.sigmoid(_dot(hn, wpg_ref[...])) * pp


def _final(dest, x1, gates_t, p, ys, wts, tile_off, n_tok, tm=TM_TOK):
    in_specs = [
        pl.BlockSpec((TOP_K, tm), lambda i: (0, i + tile_off), memory_space=pltpu.SMEM),
        pl.BlockSpec((tm, D_MODEL), lambda i: (i + tile_off, 0)),
        pl.BlockSpec((tm, TOP_K), lambda i: (i + tile_off, 0)),
        pl.BlockSpec((tm, PLE_DIM), lambda i: (i, 0)),
        pl.BlockSpec(memory_space=pl.ANY),
    ] + [_full_spec(w.shape) for w in wts]
    return pl.pallas_call(
        functools.partial(_final_body, tm=tm),
        out_shape=jax.ShapeDtypeStruct((n_tok, D_MODEL), F32),
        grid=(n_tok // tm,), in_specs=in_specs,
        out_specs=pl.BlockSpec((tm, D_MODEL), lambda i: (i, 0)),
        scratch_shapes=[pltpu.VMEM((TOP_K, tm, D_MODEL), F32), pltpu.SemaphoreType.DMA(())],
        compiler_params=pltpu.CompilerParams(dimension_semantics=("arbitrary",),
                                             vmem_limit_bytes=VMEM_LIMIT),
        name="final",
    )(dest, x1, gates_t, p, ys, *wts)


def _rope_angles(pos):
    inv = ROPE_THETA ** (-jnp.arange(ROPE_HALF, dtype=F32) / ROPE_HALF)
    ang = pos.astype(F32)[:, None] * inv
    return jnp.cos(ang), jnp.sin(ang)


def _rope_row_tables(pos):
    cos, sin = _rope_angles(pos)
    n = pos.shape[0]
    z = lambda w: jnp.zeros((n, w), F32)
    c_tab = jnp.concatenate([jnp.ones((n, QK_NOPE), F32), cos, cos, z(LANES - QK_HEAD)], axis=1)
    sa_tab = jnp.concatenate([z(QK_NOPE), -sin, z(ROPE_HALF), z(LANES - QK_HEAD)], axis=1)
    sb_tab = jnp.concatenate([z(QK_NOPE), z(ROPE_HALF), sin, z(LANES - QK_HEAD)], axis=1)
    return c_tab, sa_tab, sb_tab


def _pad_lanes(v, width=LANES):
    return jnp.pad(v, [(0, 0)] * (v.ndim - 1) + [(0, width - v.shape[-1])])


def kernel(x_prompt, x_sample, cache_ckv, cache_kpe, state_pool, page_table, p_prompt, p_sample, g_mix, w_in, g_q_a, w_q_b, g_kv_a, w_kv_b, g_q_head, g_k_head, w_pool, pool_scale, w_out, g_ffn, w_router, b_router, w_gate, b_gate, w_up, b_up, w_down, b_down, g_ple, w_ple_gate, w_ple_proj):
    assert x_prompt.shape[-1] == D_MODEL and g_mix.shape[0] == 1
    b, s, _ = x_prompt.shape
    nb, n_new, _ = x_sample.shape
    n_p, n_s = b * s, nb * n_new
    past = page_table.shape[1] * cache_ckv.shape[2]
    row = lambda v: v.reshape(1, -1).astype(F32)

    wi = w_in[0]
    w_cq, w_ckv = wi[:, :Q_LORA], wi[:, Q_LORA:Q_LORA + KV_LORA]
    w_kpe = wi[:, Q_LORA + KV_LORA:Q_LORA + KV_LORA + ROPE_DIM]
    w_u = wi[:, Q_LORA + KV_LORA + ROPE_DIM:]
    w_kpe128 = jnp.pad(w_kpe, ((0, 0), (QK_NOPE, LANES - QK_HEAD)))
    w_main = jnp.concatenate([w_cq, w_ckv, w_u, w_kpe128], axis=1).astype(BF16)
    wq = _pad_lanes(w_q_b[0]).reshape(Q_LORA, N_HEADS * LANES).astype(BF16)
    wk_nope = w_kv_b[0][:, :, :QK_NOPE]
    wk = _pad_lanes(wk_nope).reshape(KV_LORA, N_HEADS * LANES).astype(BF16)
    wv = w_kv_b[0][:, :, QK_NOPE:].reshape(KV_LORA, ATTN_WIDTH).astype(BF16)
    gq128 = _pad_lanes(row(g_q_head[0]))
    gk128 = _pad_lanes(row(g_k_head[0]))
    gkn128 = _pad_lanes(row(g_k_head[0][:QK_NOPE]))
    wkt = jnp.transpose(wk_nope, (1, 2, 0))
    wkt128 = jnp.pad(wkt, ((0, 0), (0, LANES - QK_NOPE), (0, 0))).astype(BF16)
    wkt_flat = wkt.reshape(N_HEADS * QK_NOPE, KV_LORA).astype(BF16)
    gpe_b = jnp.broadcast_to(g_k_head[0][QK_NOPE:].astype(F32)[:, None], (ROPE_DIM, LANES))
    wpool = w_pool[0].astype(BF16)
    pscale = row(pool_scale[0])
    wout_a = w_out[0][:ATTN_WIDTH].astype(BF16)
    wout_p = w_out[0][ATTN_WIDTH:].astype(BF16)
    wr_t = w_router[0].T.astype(F32)
    wr_hi = wr_t.astype(BF16)
    wr_lo = (wr_t - wr_hi.astype(F32)).astype(BF16)
    br_b = jnp.broadcast_to(b_router[0].astype(F32)[:, None], (N_EXPERTS, LANES))

    tabs_p = _rope_row_tables(jnp.arange(s, dtype=jnp.int32))
    pos_s = past + (jnp.arange(TM_PROJ, dtype=jnp.int32) % n_new)
    tabs_s = _rope_row_tables(pos_s)
    cos_t, sin_t = (t.T for t in _rope_angles(jnp.arange(past + LANES, dtype=jnp.int32)))
    tabs_attn = (cos_t[:, :past], sin_t[:, :past], cos_t[:, past:], sin_t[:, past:])

    ckv_p, kpe_p, q_p, k_p, v_p, xmid_p, pst_p = _proj_prompt(
        x_prompt, tabs_p,
        (row(g_mix[0]), w_main, row(g_q_a[0]), row(g_kv_a[0]), wq, gq128, wk, wv, gk128, wpool, pscale, wout_p))
    attn_p = _flash_prompt(q_p, k_p, v_p, b, s)

    xs_flat = x_sample.reshape(n_s, D_MODEL)
    ckv_s, kpe_s, u_s, qf_s, qlat_s = _proj_sample(
        xs_flat, tabs_s, (row(g_mix[0]), w_main, row(g_q_a[0]), row(g_kv_a[0]), wq, gq128, wkt128, gkn128))
    u_s3 = u_s.reshape(nb, n_new, POOL_WIDTH)
    pool_t = _pool_sample(jnp.transpose(state_pool[0], (1, 0, 2)), jnp.transpose(u_s3, (1, 0, 2)), wpool, pscale)
    pool_s = jnp.transpose(pool_t, (1, 0, 2)).reshape(n_s, POOL_WIDTH)
    kpe_new_t = _pad_lanes(jnp.transpose(kpe_s.reshape(nb, n_new, ROPE_DIM), (0, 2, 1)))
    attn_s = _sample_attn(page_table, cache_ckv[0], jnp.transpose(cache_kpe[0], (0, 2, 1)), qlat_s, qf_s,
                          ckv_s.reshape(nb, n_new, KV_LORA), kpe_new_t,
                          tabs_attn, (wkt_flat, gpe_b, wv)).reshape(n_s, ATTN_WIDTH)

    x1, h2, topi, gates, rank, counts = _post_attn(
        xmid_p, attn_p, xs_flat, attn_s, pool_s, (wout_a, wout_p, row(g_ffn[0]), wr_hi, wr_lo, br_b))
    nt = n_p + n_s
    counts = counts[:, 0]
    padded = (counts + TB - 1) // TB * TB
    ends = jnp.cumsum(padded)
    starts = ends - padded
    eids = jnp.arange(N_EXPERTS, dtype=jnp.int32)
    dest = (jnp.sum(jnp.where(topi[..., None] == eids, starts, 0), axis=-1) + rank).astype(jnp.int32)
    n_rows = nt * TOP_K + N_EXPERTS * TB
    n_blocks = n_rows // TB
    n_used = (ends[-1] // TB).astype(jnp.int32).reshape(1)
    blk_start = jnp.arange(n_blocks, dtype=jnp.int32) * TB
    blk_start = jnp.minimum(blk_start, jnp.maximum(ends[-1] - TB, 0))
    block_expert = jnp.minimum(jnp.sum((ends[None, :] <= blk_start[:, None]).astype(jnp.int32), axis=1),
                               N_EXPERTS - 1).astype(jnp.int32)

    row_token = _invert(dest, n_rows)
    b3 = lambda v: v[0].reshape(N_EXPERTS, 1, D_MODEL).astype(F32)
    ys = _experts(block_expert, n_used, row_token, h2, w_gate[0], b3(b_gate), w_up[0], b3(b_up), w_down[0], b3(b_down))
    gates_t = gates.T
    fin_w = (row(g_ple[0]), w_ple_gate[0].astype(BF16), w_ple_proj[0].astype(BF16))
    y_p = _final(dest, x1, gates_t, p_prompt[0].reshape(n_p, PLE_DIM), ys, fin_w, 0, n_p)
    y_s = _final(dest, x1, gates_t, p_sample[0].reshape(n_s, PLE_DIM), ys, fin_w, n_p // TM_TOK, n_s)

    pool_state_s = jnp.concatenate([state_pool[0][:, n_new:], u_s3], axis=1)
    return (y_p.reshape(b, s, D_MODEL), y_s.reshape(nb, n_new, D_MODEL),
            ckv_p.reshape(1, b, s, KV_LORA), kpe_p.reshape(1, b, s, ROPE_DIM), pst_p[:, HALO - POOL_PREV:][None],
            ckv_s.reshape(1, nb, n_new, KV_LORA), kpe_s.reshape(1, nb, n_new, ROPE_DIM), pool_state_s[None])
```

```python
import functools

import jax
import jax.numpy as jnp
from jax import lax
from jax.experimental import pallas as pl
from jax.experimental.pallas import tpu as pltpu

F32 = jnp.float32
BF16 = jnp.bfloat16

D_MODEL = 1024
N_HEADS = 8
QK_NOPE = 64
ROPE_DIM = 32
ROPE_HALF = ROPE_DIM // 2
QK_HEAD = QK_NOPE + ROPE_DIM
V_HEAD = 64
Q_LORA = 384
KV_LORA = 256
ATTN_WIDTH = N_HEADS * V_HEAD
ROPE_THETA = 10000.0
SM_SCALE = QK_HEAD ** -0.5
POOL_WINDOWS = (2, 4, 8, 16)
POOL_GROUP_DIM = 128
POOL_WIDTH = 512
POOL_PREV = 15
N_EXPERTS = 32
TOP_K = 4
SWIGLU_LIMIT = 7.0
SWIGLU_ALPHA = 1.702
PLE_DIM = 256
EPS = 1e-6

LANES = 128
HALO = 16
MAIN_COLS = Q_LORA + KV_LORA + POOL_WIDTH + LANES
VMEM_LIMIT = 56 * 1024 * 1024

TM_PROJ = 256
TQ, TK = 256, 1024
PAGES_PER_STEP = 16
RING = 3
SUB = 256
TM_TOK = 256
TM_DISP = 512
TB = 256

NEG_INF = float("-inf")


def _dot(a, b):
    return jnp.dot(a, b, preferred_element_type=F32)


def _dot_nt(a, b):
    return lax.dot_general(a, b, (((1,), (1,)), ((), ())), preferred_element_type=F32)


def _rms(x, n):
    return x * lax.rsqrt(jnp.sum(x * x, axis=-1, keepdims=True) * (1.0 / n) + EPS)


def _rope_rows(x, c_tab, sa_tab, sb_tab):
    return (x * c_tab + pltpu.roll(x, LANES - ROPE_HALF, 1) * sa_tab
            + pltpu.roll(x, ROPE_HALF, 1) * sb_tab)


def _proj_body(*refs, tm, seq_mode):
    if seq_mode:
        (x_ref, c_ref, sa_ref, sb_ref, gmix_ref, wmain_ref, gqa_ref, gkva_ref, wq_ref, gq_ref,
         wk_ref, wv_ref, gk_ref, wpool_ref, pscale_ref, woutp_ref,
         ckv_ref, kpe_ref, q_ref, k_ref, v_ref, xmid_ref, pst_ref, ubuf) = refs
    else:
        (x_ref, c_ref, sa_ref, sb_ref, gmix_ref, wmain_ref, gqa_ref, gkva_ref, wq_ref, gq_ref,
         wkt_ref, gkn_ref,
         ckv_ref, kpe_ref, u_ref, q_ref, qlat_ref) = refs

    x = x_ref[...]
    h = (_rms(x, D_MODEL) * gmix_ref[...]).astype(BF16)
    z = _dot(h, wmain_ref[...])
    cqn = (_rms(z[:, :Q_LORA], Q_LORA) * gqa_ref[...]).astype(BF16)
    ckvn = _rms(z[:, Q_LORA:Q_LORA + KV_LORA], KV_LORA) * gkva_ref[...]
    u = z[:, Q_LORA + KV_LORA:Q_LORA + KV_LORA + POOL_WIDTH]
    kpe128 = z[:, Q_LORA + KV_LORA + POOL_WIDTH:]
    ckv_ref[...] = ckvn
    kpe_ref[...] = pltpu.roll(kpe128, LANES - QK_NOPE, 1)[:, :ROPE_DIM]

    c_tab, sa_tab, sb_tab = c_ref[...], sa_ref[...], sb_ref[...]
    q_all = _dot(cqn, wq_ref[...])
    gq = gq_ref[...]
    for hd in range(N_HEADS):
        qh = q_all[:, hd * LANES:(hd + 1) * LANES]
        qh = _rope_rows(_rms(qh, QK_HEAD) * gq, c_tab, sa_tab, sb_tab) * SM_SCALE
        if seq_mode:
            q_ref[hd] = qh.astype(BF16)
        else:
            q_ref[hd] = qh
            qlat_ref[hd] = _dot((qh * gkn_ref[...]).astype(BF16), wkt_ref[hd])

    if not seq_mode:
        u_ref[...] = u
        return

    ckvb = ckvn.astype(BF16)
    k_all = _dot(ckvb, wk_ref[...])
    gk = gk_ref[...]
    for hd in range(N_HEADS):
        kh = k_all[:, hd * LANES:(hd + 1) * LANES] + kpe128
        kh = _rope_rows(_rms(kh, QK_HEAD) * gk, c_tab, sa_tab, sb_tab)
        k_ref[hd] = kh.astype(BF16)
    v_all = _dot(ckvb, wv_ref[...])
    for hp in range(N_HEADS // 2):
        v_ref[hp] = v_all[:, hp * LANES:(hp + 1) * LANES].astype(BF16)

    j = pl.program_id(1)

    @pl.when(j == 0)
    def _():
        ubuf[0:HALO, :] = jnp.zeros((HALO, POOL_WIDTH), F32)

    ubuf[HALO:HALO + tm, :] = u
    pos1 = (j * tm + 1 + lax.broadcasted_iota(jnp.int32, (tm, 1), 0)).astype(F32)
    pool_parts = []
    for g, w in enumerate(POOL_WINDOWS):
        sl = slice(g * POOL_GROUP_DIM, (g + 1) * POOL_GROUP_DIM)
        ug = u[:, sl]
        acc = ug
        for back in range(1, w):
            acc = acc + ubuf[HALO - back:HALO - back + tm, sl]
        d = acc / jnp.minimum(float(w), pos1) - ug
        pool_parts.append(_dot(d.astype(BF16), wpool_ref[g]) * pscale_ref[:, sl])
    pool_out = jnp.concatenate(pool_parts, axis=1).astype(BF16)
    xmid_ref[...] = x + _dot(pool_out, woutp_ref[...])

    tail = ubuf[tm:tm + HALO, :]
    ubuf[0:HALO, :] = tail

    @pl.when(j == pl.num_programs(1) - 1)
    def _():
        pst_ref[...] = tail


def _full_spec(shape):
    nd = len(shape)
    return pl.BlockSpec(shape, lambda *_: (0,) * nd)


def _proj_prompt(x, tabs, wts, tm=TM_PROJ):
    b, s, _ = x.shape
    n = b * s
    nj = s // tm
    c_tab, sa_tab, sb_tab = tabs
    row = lambda width: pl.BlockSpec((tm, width), lambda bi, j: (bi * nj + j, 0))
    tab = pl.BlockSpec((tm, LANES), lambda bi, j: (j, 0))
    heads = lambda nh, width: pl.BlockSpec((nh, tm, width), lambda bi, j: (0, bi * nj + j, 0))
    in_specs = [pl.BlockSpec((None, tm, D_MODEL), lambda bi, j: (bi, j, 0)), tab, tab, tab] + [
        _full_spec(w.shape) for w in wts]
    out_shape = (
        jax.ShapeDtypeStruct((n, KV_LORA), F32),
        jax.ShapeDtypeStruct((n, ROPE_DIM), F32),
        jax.ShapeDtypeStruct((N_HEADS, n, LANES), BF16),
        jax.ShapeDtypeStruct((N_HEADS, n, LANES), BF16),
        jax.ShapeDtypeStruct((N_HEADS // 2, n, LANES), BF16),
        jax.ShapeDtypeStruct((n, D_MODEL), F32),
        jax.ShapeDtypeStruct((b, HALO, POOL_WIDTH), F32),
    )
    out_specs = (row(KV_LORA), row(ROPE_DIM), heads(N_HEADS, LANES), heads(N_HEADS, LANES),
                 heads(N_HEADS // 2, LANES), row(D_MODEL),
                 pl.BlockSpec((None, HALO, POOL_WIDTH), lambda bi, j: (bi, 0, 0)))
    return pl.pallas_call(
        functools.partial(_proj_body, tm=tm, seq_mode=True),
        out_shape=out_shape, grid=(b, nj), in_specs=in_specs, out_specs=out_specs,
        scratch_shapes=[pltpu.VMEM((tm + HALO, POOL_WIDTH), F32)],
        compiler_params=pltpu.CompilerParams(dimension_semantics=("arbitrary", "arbitrary"),
                                             vmem_limit_bytes=VMEM_LIMIT),
        name="proj_prompt",
    )(x, c_tab, sa_tab, sb_tab, *wts)


def _proj_sample(x, tabs, wts, tm=TM_PROJ):
    n = x.shape[0]
    c_tab, sa_tab, sb_tab = tabs
    row = lambda width: pl.BlockSpec((tm, width), lambda i: (i, 0))
    heads = lambda width: pl.BlockSpec((N_HEADS, tm, width), lambda i: (0, i, 0))
    in_specs = [row(D_MODEL), _full_spec((tm, LANES)), _full_spec((tm, LANES)), _full_spec((tm, LANES))] + [
        _full_spec(w.shape) for w in wts]
    out_shape = (
        jax.ShapeDtypeStruct((n, KV_LORA), F32),
        jax.ShapeDtypeStruct((n, ROPE_DIM), F32),
        jax.ShapeDtypeStruct((n, POOL_WIDTH), F32),
        jax.ShapeDtypeStruct((N_HEADS, n, LANES), F32),
        jax.ShapeDtypeStruct((N_HEADS, n, KV_LORA), F32),
    )
    out_specs = (row(KV_LORA), row(ROPE_DIM), row(POOL_WIDTH), heads(LANES), heads(KV_LORA))
    return pl.pallas_call(
        functools.partial(_proj_body, tm=tm, seq_mode=False),
        out_shape=out_shape, grid=(n // tm,), in_specs=in_specs, out_specs=out_specs,
        compiler_params=pltpu.CompilerParams(dimension_semantics=("arbitrary",),
                                             vmem_limit_bytes=VMEM_LIMIT),
        name="proj_sample",
    )(x, c_tab, sa_tab, sb_tab, *wts)


def _flash_body(q_ref, k_ref, v_ref, o_ref, m_sc, l_sc, acc_sc, s_sc, *, tq, tk):
    qi = pl.program_id(2)
    n_full = (qi * tq) // tk
    row = qi * tq + lax.broadcasted_iota(jnp.int32, (tq, tk), 0)
    col = lax.broadcasted_iota(jnp.int32, (tq, tk), 1)
    m_sc[...] = jnp.full(m_sc.shape, NEG_INF, F32)
    l_sc[...] = jnp.zeros(l_sc.shape, F32)
    acc_sc[...] = jnp.zeros(acc_sc.shape, F32)

    def qk(ki, masked):
        start = pl.multiple_of(ki * tk, tk)
        par = ki % 2
        for hh in range(2):
            s = _dot_nt(q_ref[hh], k_ref[hh, pl.ds(start, tk), :])
            if masked:
                s = jnp.where(col + start <= row, s, NEG_INF)
            s_sc[par, hh] = s

    def spv(ki):
        start = pl.multiple_of(ki * tk, tk)
        par = ki % 2
        v = v_ref[pl.ds(start, tk), :]
        for hh in range(2):
            s = s_sc[par, hh]
            m_prev = m_sc[hh]
            m_new = jnp.maximum(m_prev, jnp.max(s, axis=-1, keepdims=True))
            alpha = jnp.exp(m_prev - m_new)
            p = jnp.exp(s - m_new)
            l_sc[hh] = alpha * l_sc[hh] + jnp.sum(p, axis=-1, keepdims=True)
            acc_sc[hh] = alpha * acc_sc[hh] + _dot(p.astype(BF16), v)
            m_sc[hh] = m_new

    @pl.when(n_full > 0)
    def _():
        qk(0, False)

        def loop_body(ki, carry):
            spv(ki)
            qk(ki + 1, False)
            return carry

        lax.fori_loop(0, n_full - 1, loop_body, 0)
        spv(n_full - 1)
        qk(n_full, True)

    @pl.when(n_full == 0)
    def _():
        qk(0, True)

    spv(n_full)
    lane = lax.broadcasted_iota(jnp.int32, (tq, LANES), 1)
    o_ref[...] = jnp.where(lane < V_HEAD, acc_sc[0] / l_sc[0], acc_sc[1] / l_sc[1]).astype(o_ref.dtype)


def _flash_prompt(q, k, v, b, s, tq=TQ, tk=TK):
    n = b * s
    nq = s // tq
    return pl.pallas_call(
        functools.partial(_flash_body, tq=tq, tk=tk),
        out_shape=jax.ShapeDtypeStruct((n, ATTN_WIDTH), BF16),
        grid=(b, N_HEADS // 2, nq),
        in_specs=[
            pl.BlockSpec((2, tq, LANES), lambda bi, hp, qi: (hp, bi * nq + qi, 0)),
            pl.BlockSpec((2, s, LANES), lambda bi, hp, qi: (hp, bi, 0)),
            pl.BlockSpec((None, s, LANES), lambda bi, hp, qi: (hp, bi, 0)),
        ],
        out_specs=pl.BlockSpec((tq, LANES), lambda bi, hp, qi: (bi * nq + qi, hp)),
        scratch_shapes=[pltpu.VMEM((2, tq, 1), F32), pltpu.VMEM((2, tq, 1), F32), pltpu.VMEM((2, tq, LANES), F32),
                        pltpu.VMEM((2, 2, tq, tk), F32)],
        compiler_params=pltpu.CompilerParams(dimension_semantics=("arbitrary",) * 3,
                                             vmem_limit_bytes=VMEM_LIMIT),
        name="flash_prompt",
    )(q, k, v)


def _pool_sample_body(st_ref, u_ref, wpool_ref, pscale_ref, o_ref, *, n_new):
    for t in range(n_new):
        for g, w in enumerate(POOL_WINDOWS):
            sl = slice(g * POOL_GROUP_DIM, (g + 1) * POOL_GROUP_DIM)
            ug = u_ref[t, :, sl]
            acc = ug
            for back in range(1, w):
                src = t - back
                acc = acc + (u_ref[src, :, sl] if src >= 0 else st_ref[POOL_PREV + src, :, sl])
            d = acc / float(w) - ug
            o_ref[t, :, sl] = _dot(d.astype(BF16), wpool_ref[g]) * pscale_ref[:, sl]


def _pool_sample(state_t, u_t, wpool, pscale):
    n_new, nb, _ = u_t.shape
    return pl.pallas_call(
        functools.partial(_pool_sample_body, n_new=n_new),
        out_shape=jax.ShapeDtypeStruct((n_new, nb, POOL_WIDTH), F32),
        grid=(1,),
        in_specs=[_full_spec(state_t.shape), _full_spec(u_t.shape), _full_spec(wpool.shape),
                  _full_spec(pscale.shape)],
        out_specs=_full_spec((n_new, nb, POOL_WIDTH)),
        compiler_params=pltpu.CompilerParams(dimension_semantics=("arbitrary",),
                                             vmem_limit_bytes=VMEM_LIMIT),
        name="pool_sample",
    )(state_t, u_t, wpool, pscale)


def _sample_attn_body(pt_ref, ckv_hbm, kpe_hbm, qlat_ref, q_ref, ckvn_ref, kpen_ref, cos_ref, sin_ref,
                      cost_ref, sint_ref, wkt_ref, gpe_ref, wv_ref, o_ref,
                      cbuf, kbuf, sem_c, sem_k, a_sc, m_sc, l_sc, acc_sc, cbb0, cbb1, sb0, sb1,
                      *, n_pages, n_new):
    npg = PAGES_PER_STEP
    page = LANES
    n_chunks = n_pages // npg
    b = pl.program_id(0)
    nb = pl.num_programs(0)
    rows = N_HEADS * n_new

    def page_copies(seq, chunk, slot):
        cps = []
        for i in range(npg):
            pg = pt_ref[seq, chunk * npg + i]
            cps.append(pltpu.make_async_copy(ckv_hbm.at[pg], cbuf.at[slot, pl.ds(i * page, page)], sem_c.at[slot]))
            cps.append(pltpu.make_async_copy(kpe_hbm.at[pg], kbuf.at[slot, i], sem_k.at[slot]))
        return cps

    def start_chunk(seq, chunk, slot):
        for cp in page_copies(seq, chunk, slot):
            cp.start()

    def wait_chunk(slot):
        for cp in page_copies(0, 0, slot):
            cp.wait()

    total = nb * n_chunks

    @pl.when(b == 0)
    def _():
        for g0 in range(RING - 1):
            start_chunk(g0 // n_chunks, g0 % n_chunks, g0 % RING)

    a_sc[0:N_HEADS * QK_NOPE, :] = wkt_ref[...]
    a_sc[N_HEADS * QK_NOPE:, :] = qlat_ref[...].reshape(rows, KV_LORA).astype(BF16)
    m_sc[...] = jnp.full((rows, LANES), NEG_INF, F32)
    l_sc[...] = jnp.zeros((rows, LANES), F32)
    acc_sc[...] = jnp.zeros((rows, KV_LORA), F32)
    qpe = q_ref[...].reshape(rows, LANES)[:, QK_NOPE:QK_HEAD].astype(BF16)
    g1 = gpe_ref[0:ROPE_HALF, :]
    g2 = gpe_ref[ROPE_HALF:ROPE_DIM, :]

    def scores(cb, kpt, cos, sin):
        ck = cb.shape[0]
        r = _dot_nt(a_sc[...], cb)
        kt = r[:N_HEADS * QK_NOPE]
        ssn = jnp.sum((kt * kt).reshape(N_HEADS, QK_NOPE, ck), axis=1)
        x1 = kpt[0:ROPE_HALF]
        x2 = kpt[ROPE_HALF:ROPE_DIM]
        ssp = jnp.sum(x1 * x1 + x2 * x2, axis=0, keepdims=True)
        rs = lax.rsqrt((ssn + ssp) * (1.0 / QK_HEAD) + EPS)
        reps = ck // LANES
        x1 = x1 * jnp.tile(g1, (1, reps))
        x2 = x2 * jnp.tile(g2, (1, reps))
        rot = jnp.concatenate([x1 * cos - x2 * sin, x2 * cos + x1 * sin], axis=0).astype(BF16)
        s = r[N_HEADS * QK_NOPE:] + _dot(qpe, rot)
        s = s.reshape(N_HEADS, n_new, ck) * rs[:, None, :]
        return s.reshape(rows, ck)

    def update(s_all, cbs):
        m_prev = m_sc[:, :1]
        m_new = jnp.maximum(m_prev, jnp.max(s_all, axis=-1, keepdims=True))
        alpha = jnp.exp(m_prev - m_new)
        p = jnp.exp(s_all - m_new)
        l_sc[...] = jnp.broadcast_to(alpha * l_sc[:, :1] + jnp.sum(p, axis=-1, keepdims=True),
                                     (rows, LANES))
        acc = alpha * acc_sc[...]
        off = 0
        for cb in cbs:
            ck = cb.shape[0]
            acc = acc + _dot(p[:, off:off + ck].astype(BF16), cb)
            off += ck
        acc_sc[...] = acc
        m_sc[...] = jnp.broadcast_to(m_new, (rows, LANES))

    pages_per_sub = SUB // page

    n_sub = npg // pages_per_sub

    def scores_chunk(jc, cbb, s_buf):
        g = b * n_chunks + jc
        slot = g % RING
        wait_chunk(slot)
        nxt = jnp.minimum(g + RING - 1, total - 1)
        start_chunk(nxt // n_chunks, nxt % n_chunks, (g + RING - 1) % RING)
        for sc in range(n_sub):
            cb = cbuf[slot, sc * SUB:(sc + 1) * SUB, :].astype(BF16)
            cbb[sc * SUB:(sc + 1) * SUB, :] = cb
            kpt = jnp.concatenate([kbuf[slot, sc * pages_per_sub + i] for i in range(pages_per_sub)], axis=1)
            start = pl.multiple_of(jc * (npg * page) + sc * SUB, SUB)
            s_buf[:, sc * SUB:(sc + 1) * SUB] = scores(
                cb, kpt, cos_ref[:, pl.ds(start, SUB)], sin_ref[:, pl.ds(start, SUB)])

    def update_chunk(cbb, s_buf):
        update(s_buf[...], [cbb[sc * SUB:(sc + 1) * SUB, :] for sc in range(n_sub)])

    scores_chunk(0, cbb0, sb0)

    pad = LANES - n_new
    cb_new = jnp.concatenate([ckvn_ref[...], jnp.zeros((pad, KV_LORA), F32)], axis=0).astype(BF16)
    s_new = scores(cb_new, kpen_ref[...], cost_ref[...], sint_ref[...])
    key = lax.broadcasted_iota(jnp.int32, (rows, LANES), 1)
    tok = lax.broadcasted_iota(jnp.int32, (rows, LANES), 0) % n_new
    update(jnp.where(key <= tok, s_new, NEG_INF), [cb_new])

    def pair(i, carry):
        scores_chunk(2 * i + 1, cbb1, sb1)
        update_chunk(cbb0, sb0)
        scores_chunk(2 * i + 2, cbb0, sb0)
        update_chunk(cbb1, sb1)
        return carry

    lax.fori_loop(0, n_chunks // 2 - 1, pair, 0)
    scores_chunk(n_chunks - 1, cbb1, sb1)
    update_chunk(cbb0, sb0)
    update_chunk(cbb1, sb1)

    @pl.when(b == nb - 1)
    def _():
        for extra in range(RING - 1):
            wait_chunk((total + extra) % RING)

    o_lat = (acc_sc[...] / l_sc[:, :1]).astype(BF16)
    full = _dot(o_lat, wv_ref[...]).reshape(N_HEADS, n_new, ATTN_WIDTH)
    hd = lax.broadcasted_iota(jnp.int32, (N_HEADS, n_new, ATTN_WIDTH), 0)
    colh = lax.broadcasted_iota(jnp.int32, (N_HEADS, n_new, ATTN_WIDTH), 2) // V_HEAD
    o_ref[...] = jnp.sum(jnp.where(hd == colh, full, 0.0), axis=0)


def _sample_attn(page_table, cache_ckv, cache_kpe_t, qlat, qf, ckv_new, kpe_new_t, tabs, wts):
    nb, n_pages = page_table.shape
    page = cache_ckv.shape[1]
    n_new = ckv_new.shape[1]
    npg = PAGES_PER_STEP
    assert page == LANES and n_pages % (2 * npg) == 0 and nb * (n_pages // npg) >= RING
    cos_t, sin_t, cos_tail, sin_tail = tabs
    rows = N_HEADS * n_new
    const = lambda shape: pl.BlockSpec(shape, lambda b, pt, nd=len(shape): (0,) * nd)
    in_specs = [
        pl.BlockSpec(memory_space=pl.ANY), pl.BlockSpec(memory_space=pl.ANY),
        pl.BlockSpec((N_HEADS, n_new, KV_LORA), lambda b, pt: (0, b, 0)),
        pl.BlockSpec((N_HEADS, n_new, LANES), lambda b, pt: (0, b, 0)),
        pl.BlockSpec((None, n_new, KV_LORA), lambda b, pt: (b, 0, 0)),
        pl.BlockSpec((None, ROPE_DIM, LANES), lambda b, pt: (b, 0, 0)),
        const(cos_t.shape), const(sin_t.shape), const(cos_tail.shape), const(sin_tail.shape),
    ] + [const(w.shape) for w in wts]
    grid_spec = pltpu.PrefetchScalarGridSpec(
        num_scalar_prefetch=1, grid=(nb,), in_specs=in_specs,
        out_specs=pl.BlockSpec((None, n_new, ATTN_WIDTH), lambda b, pt: (b, 0, 0)),
        scratch_shapes=[pltpu.VMEM((RING, npg * page, KV_LORA), F32),
                        pltpu.VMEM((RING, npg, ROPE_DIM, page), F32),
                        pltpu.SemaphoreType.DMA((RING,)), pltpu.SemaphoreType.DMA((RING,)),
                        pltpu.VMEM((N_HEADS * QK_NOPE + rows, KV_LORA), BF16),
                        pltpu.VMEM((rows, LANES), F32), pltpu.VMEM((rows, LANES), F32),
                        pltpu.VMEM((rows, KV_LORA), F32),
                        pltpu.VMEM((npg * page, KV_LORA), BF16), pltpu.VMEM((npg * page, KV_LORA), BF16),
                        pltpu.VMEM((rows, npg * page), F32), pltpu.VMEM((rows, npg * page), F32)])
    return pl.pallas_call(
        functools.partial(_sample_attn_body, n_pages=n_pages, n_new=n_new),
        out_shape=jax.ShapeDtypeStruct((nb, n_new, ATTN_WIDTH), F32),
        grid_spec=grid_spec,
        compiler_params=pltpu.CompilerParams(dimension_semantics=("arbitrary",),
                                             vmem_limit_bytes=VMEM_LIMIT),
        name="sample_attn",
    )(page_table, cache_ckv, cache_kpe_t, qlat, qf, ckv_new, kpe_new_t, cos_t, sin_t, cos_tail, sin_tail, *wts)


def _post_attn_body(xmid_ref, attn_p_ref, xs_ref, attn_s_ref, pool_s_ref, wouta_ref, woutp_ref, gffn_ref,
                    wrh_ref, wrl_ref, br_ref,
                    x1_ref, h2_ref, topi_ref, gate_ref, rank_ref, cnt_ref, base_sc, xm_sc, at_sc,
                    *, tm, n_prompt_tiles):
    i = pl.program_id(0)

    @pl.when(i == 0)
    def _():
        base_sc[...] = jnp.zeros((N_EXPERTS, LANES), F32)

    @pl.when(i < n_prompt_tiles)
    def _():
        xm_sc[...] = xmid_ref[...]
        at_sc[...] = attn_p_ref[...]

    @pl.when(i >= n_prompt_tiles)
    def _():
        xm_sc[...] = xs_ref[...] + _dot(pool_s_ref[...].astype(BF16), woutp_ref[...])
        at_sc[...] = attn_s_ref[...].astype(BF16)

    x1 = xm_sc[...] + _dot(at_sc[...], wouta_ref[...])
    x1_ref[...] = x1
    h2 = _rms(x1, D_MODEL) * gffn_ref[...]
    h2_ref[...] = h2
    h_hi = h2.astype(BF16)
    h_lo = (h2 - h_hi.astype(F32)).astype(BF16)
    lg = (_dot_nt(wrh_ref[...], h_hi) + _dot_nt(wrh_ref[...], h_lo) + _dot_nt(wrl_ref[...], h_hi)
          + br_ref[:, :1])
    eid = lax.broadcasted_iota(jnp.int32, (N_EXPERTS, tm), 0)
    vals, idxs, hots = [], [], []
    for _ in range(TOP_K):
        mx = jnp.max(lg, axis=0, keepdims=True)
        ix = jnp.min(jnp.where(lg == mx, eid, N_EXPERTS), axis=0, keepdims=True)
        hot = eid == ix
        lg = jnp.where(hot, NEG_INF, lg)
        vals.append(mx)
        idxs.append(ix)
        hots.append(hot)
    ex = [jnp.exp(v - vals[0]) for v in vals]
    den = ex[0] + ex[1] + ex[2] + ex[3]
    gate_ref[...] = jnp.concatenate([e / den for e in ex], axis=0)
    topi_ref[...] = jnp.concatenate(idxs, axis=0)

    anyhot = (hots[0] | hots[1] | hots[2] | hots[3])
    hot_f = jnp.where(anyhot, 1.0, 0.0)
    upper = jnp.where(lax.broadcasted_iota(jnp.int32, (tm, tm), 0) < lax.broadcasted_iota(jnp.int32, (tm, tm), 1),
                      1.0, 0.0).astype(BF16)
    before = _dot(hot_f.astype(BF16), upper) + base_sc[:, :1]
    ranks = [jnp.sum(jnp.where(h, before, 0.0), axis=0, keepdims=True) for h in hots]
    rank_ref[...] = jnp.concatenate(ranks, axis=0).astype(jnp.int32)
    base_sc[...] = base_sc[...] + jnp.sum(hot_f, axis=1, keepdims=True)
    cnt_ref[...] = base_sc[...].astype(jnp.int32)


def _post_attn(xmid_p, attn_p, x_s, attn_s, pool_s, wts, tm=TM_TOK):
    n_p, n_s = xmid_p.shape[0], x_s.shape[0]
    npt, nst = n_p // tm, n_s // tm
    nt = n_p + n_s
    p_idx = lambda i: (jnp.minimum(i, npt - 1), 0)
    s_idx = lambda i: (jnp.maximum(i - npt, 0), 0)
    in_specs = [
        pl.BlockSpec((tm, D_MODEL), p_idx), pl.BlockSpec((tm, ATTN_WIDTH), p_idx),
        pl.BlockSpec((tm, D_MODEL), s_idx), pl.BlockSpec((tm, ATTN_WIDTH), s_idx),
        pl.BlockSpec((tm, POOL_WIDTH), s_idx),
    ] + [_full_spec(w.shape) for w in wts]
    col = lambda rows: pl.BlockSpec((rows, tm), lambda i: (0, i))
    out_shape = (
        jax.ShapeDtypeStruct((nt, D_MODEL), F32), jax.ShapeDtypeStruct((nt, D_MODEL), F32),
        jax.ShapeDtypeStruct((TOP_K, nt), jnp.int32), jax.ShapeDtypeStruct((TOP_K, nt), F32),
        jax.ShapeDtypeStruct((TOP_K, nt), jnp.int32), jax.ShapeDtypeStruct((N_EXPERTS, LANES), jnp.int32),
    )
    out_specs = (pl.BlockSpec((tm, D_MODEL), lambda i: (i, 0)), pl.BlockSpec((tm, D_MODEL), lambda i: (i, 0)),
                 col(TOP_K), col(TOP_K), col(TOP_K), _full_spec((N_EXPERTS, LANES)))
    return pl.pallas_call(
        functools.partial(_post_attn_body, tm=tm, n_prompt_tiles=npt),
        out_shape=out_shape, grid=(npt + nst,), in_specs=in_specs, out_specs=out_specs,
        scratch_shapes=[pltpu.VMEM((N_EXPERTS, LANES), F32), pltpu.VMEM((tm, D_MODEL), F32),
                        pltpu.VMEM((tm, ATTN_WIDTH), BF16)],
        compiler_params=pltpu.CompilerParams(dimension_semantics=("arbitrary",),
                                             vmem_limit_bytes=VMEM_LIMIT),
        name="post_attn",
    )(xmid_p, attn_p, x_s, attn_s, pool_s, *wts)


def _invert_body(dest_ref, zeros_hbm, rt_ref, sem, *, tm):
    i = pl.program_id(0)

    @pl.when(i == 0)
    def _():
        cp = pltpu.make_async_copy(zeros_hbm, rt_ref, sem)
        cp.start()
        cp.wait()

    base = i * tm

    for t in range(tm):
        for k in range(TOP_K):
            rt_ref[dest_ref[k, t]] = base + t


def _invert(dest, n_rows, tm=TM_DISP):
    nt = dest.shape[1]
    return pl.pallas_call(
        functools.partial(_invert_body, tm=tm),
        out_shape=jax.ShapeDtypeStruct((n_rows,), jnp.int32),
        grid=(nt // tm,),
        in_specs=[pl.BlockSpec((TOP_K, tm), lambda i: (0, i), memory_space=pltpu.SMEM),
                  pl.BlockSpec(memory_space=pl.ANY)],
        out_specs=pl.BlockSpec(memory_space=pltpu.SMEM),
        scratch_shapes=[pltpu.SemaphoreType.DMA(())],
        compiler_params=pltpu.CompilerParams(dimension_semantics=("arbitrary",)),
        name="invert",
    )(dest, jnp.zeros((n_rows,), jnp.int32))


def _experts_body(be_ref, nu_ref, rt_cur_ref, rt_nxt_ref, h2_ref, wg_ref, bg_ref, wu_ref, bu_ref, wd_ref, bd_ref,
                  y_ref, xbuf, sem, wgb, wub, wdb, *, tb):
    blk = pl.program_id(0)
    slot = blk % 2
    n_used = nu_ref[0]

    @pl.when((blk == 0) | (be_ref[blk] != be_ref[jnp.maximum(blk - 1, 0)]))
    def _():
        wgb[...] = wg_ref[...].astype(BF16)
        wub[...] = wu_ref[...].astype(BF16)
        wdb[...] = wd_ref[...].astype(BF16)

    def gather(rt_ref, s):
        for r in range(tb):
            pltpu.make_async_copy(h2_ref.at[pl.ds(rt_ref[r], 1)], xbuf.at[s, pl.ds(r, 1)],
                                  sem.at[s]).start(priority=r % 2)

    @pl.when((blk == 0) & (n_used > 0))
    def _():
        gather(rt_cur_ref, 0)

    @pl.when(blk + 1 < n_used)
    def _():
        gather(rt_nxt_ref, 1 - slot)

    @pl.when(blk < n_used)
    def _():
        pltpu.make_async_copy(h2_ref.at[pl.ds(0, tb)], xbuf.at[slot], sem.at[slot]).wait()
        xb = xbuf[slot].astype(BF16)
        g = jnp.minimum(_dot(xb, wgb[...]) + bg_ref[...], SWIGLU_LIMIT)
        u = jnp.clip(_dot(xb, wub[...]) + bu_ref[...], -SWIGLU_LIMIT, SWIGLU_LIMIT)
        act = (u + 1.0) * (g * jax.nn.sigmoid(SWIGLU_ALPHA * g))
        y_ref[...] = _dot(act.astype(BF16), wdb[...]) + bd_ref[...]

    @pl.when(blk >= n_used)
    def _():
        y_ref[...] = jnp.zeros(y_ref.shape, F32)


def _experts(block_expert, n_used, row_token, h2, wg, bg, wu, bu, wd, bd, tb=TB):
    r = row_token.shape[0]
    nblk = r // tb
    w_spec = pl.BlockSpec((None, D_MODEL, D_MODEL), lambda b, be, nu: (be[b], 0, 0))
    b_spec = pl.BlockSpec((None, 1, D_MODEL), lambda b, be, nu: (be[b], 0, 0))
    rt_cur = pl.BlockSpec((tb,), lambda b, be, nu: (b,), memory_space=pltpu.SMEM)
    rt_nxt = pl.BlockSpec((tb,), lambda b, be, nu: (jnp.minimum(b + 1, nblk - 1),), memory_space=pltpu.SMEM)
    grid_spec = pltpu.PrefetchScalarGridSpec(
        num_scalar_prefetch=2, grid=(nblk,),
        in_specs=[rt_cur, rt_nxt, pl.BlockSpec(memory_space=pl.ANY),
                  w_spec, b_spec, w_spec, b_spec, w_spec, b_spec],
        out_specs=pl.BlockSpec((tb, D_MODEL), lambda b, be, nu: (b, 0)),
        scratch_shapes=[pltpu.VMEM((2, tb, D_MODEL), F32), pltpu.SemaphoreType.DMA((2,))]
        + [pltpu.VMEM((D_MODEL, D_MODEL), BF16)] * 3)
    return pl.pallas_call(
        functools.partial(_experts_body, tb=tb),
        out_shape=jax.ShapeDtypeStruct((r, D_MODEL), F32), grid_spec=grid_spec,
        compiler_params=pltpu.CompilerParams(dimension_semantics=("arbitrary",),
                                             vmem_limit_bytes=VMEM_LIMIT),
        name="experts",
    )(block_expert, n_used, row_token, row_token, h2, wg, bg, wu, bu, wd, bd)


def _final_body(dest_ref, x1_ref, gate_ref, p_ref, ys_ref, gple_ref, wpg_ref, wpp_ref, o_ref, gbuf, sem, *, tm):
    for t in range(tm):
        for k in range(TOP_K):
            pltpu.make_async_copy(ys_ref.at[pl.ds(dest_ref[k, t], 1)], gbuf.at[k, pl.ds(t, 1)],
                                  sem).start(priority=k % 2)
    pp = _dot(p_ref[...].astype(BF16), wpp_ref[...])
    for k in range(TOP_K):
        pltpu.make_async_copy(ys_ref.at[pl.ds(0, tm)], gbuf.at[k], sem).wait()
    gates = gate_ref[...]
    x2 = x1_ref[...]
    for k in range(TOP_K):
        x2 = x2 + gates[:, k:k + 1] * gbuf[k]
    hn = (_rms(x2, D_MODEL) * gple_ref[...]).astype(BF16)
    o_ref[...] = x2 + jax.nn.sigmoid(_dot(hn, wpg_ref[...])) * pp


def _final(dest, x1, gates_t, p, ys, wts, tile_off, n_tok, tm=TM_TOK):
    in_specs = [
        pl.BlockSpec((TOP_K, tm), lambda i: (0, i + tile_off), memory_space=pltpu.SMEM),
        pl.BlockSpec((tm, D_MODEL), lambda i: (i + tile_off, 0)),
        pl.BlockSpec((tm, TOP_K), lambda i: (i + tile_off, 0)),
        pl.BlockSpec((tm, PLE_DIM), lambda i: (i, 0)),
        pl.BlockSpec(memory_space=pl.ANY),
    ] + [_full_spec(w.shape) for w in wts]
    return pl.pallas_call(
        functools.partial(_final_body, tm=tm),
        out_shape=jax.ShapeDtypeStruct((n_tok, D_MODEL), F32),
        grid=(n_tok // tm,), in_specs=in_specs,
        out_specs=pl.BlockSpec((tm, D_MODEL), lambda i: (i, 0)),
        scratch_shapes=[pltpu.VMEM((TOP_K, tm, D_MODEL), F32), pltpu.SemaphoreType.DMA(())],
        compiler_params=pltpu.CompilerParams(dimension_semantics=("arbitrary",),
                                             vmem_limit_bytes=VMEM_LIMIT),
        name="final",
    )(dest, x1, gates_t, p, ys, *wts)


def _rope_angles(pos):
    inv = ROPE_THETA ** (-jnp.arange(ROPE_HALF, dtype=F32) / ROPE_HALF)
    ang = pos.astype(F32)[:, None] * inv
    return jnp.cos(ang), jnp.sin(ang)


def _rope_row_tables(pos):
    cos, sin = _rope_angles(pos)
    n = pos.shape[0]
    z = lambda w: jnp.zeros((n, w), F32)
    c_tab = jnp.concatenate([jnp.ones((n, QK_NOPE), F32), cos, cos, z(LANES - QK_HEAD)], axis=1)
    sa_tab = jnp.concatenate([z(QK_NOPE), -sin, z(ROPE_HALF), z(LANES - QK_HEAD)], axis=1)
    sb_tab = jnp.concatenate([z(QK_NOPE), z(ROPE_HALF), sin, z(LANES - QK_HEAD)], axis=1)
    return c_tab, sa_tab, sb_tab


def _pad_lanes(v, width=LANES):
    return jnp.pad(v, [(0, 0)] * (v.ndim - 1) + [(0, width - v.shape[-1])])


def kernel(x_prompt, x_sample, cache_ckv, cache_kpe, state_pool, page_table, p_prompt, p_sample, g_mix, w_in, g_q_a, w_q_b, g_kv_a, w_kv_b, g_q_head, g_k_head, w_pool, pool_scale, w_out, g_ffn, w_router, b_router, w_gate, b_gate, w_up, b_up, w_down, b_down, g_ple, w_ple_gate, w_ple_proj):
    assert x_prompt.shape[-1] == D_MODEL and g_mix.shape[0] == 1
    b, s, _ = x_prompt.shape
    nb, n_new, _ = x_sample.shape
    n_p, n_s = b * s, nb * n_new
    past = page_table.shape[1] * cache_ckv.shape[2]
    row = lambda v: v.reshape(1, -1).astype(F32)

    wi = w_in[0]
    w_cq, w_ckv = wi[:, :Q_LORA], wi[:, Q_LORA:Q_LORA + KV_LORA]
    w_kpe = wi[:, Q_LORA + KV_LORA:Q_LORA + KV_LORA + ROPE_DIM]
    w_u = wi[:, Q_LORA + KV_LORA + ROPE_DIM:]
    w_kpe128 = jnp.pad(w_kpe, ((0, 0), (QK_NOPE, LANES - QK_HEAD)))
    w_main = jnp.concatenate([w_cq, w_ckv, w_u, w_kpe128], axis=1).astype(BF16)
    wq = _pad_lanes(w_q_b[0]).reshape(Q_LORA, N_HEADS * LANES).astype(BF16)
    wk_nope = w_kv_b[0][:, :, :QK_NOPE]
    wk = _pad_lanes(wk_nope).reshape(KV_LORA, N_HEADS * LANES).astype(BF16)
    wv = w_kv_b[0][:, :, QK_NOPE:].reshape(KV_LORA, ATTN_WIDTH).astype(BF16)
    gq128 = _pad_lanes(row(g_q_head[0]))
    gk128 = _pad_lanes(row(g_k_head[0]))
    gkn128 = _pad_lanes(row(g_k_head[0][:QK_NOPE]))
    wkt = jnp.transpose(wk_nope, (1, 2, 0))
    wkt128 = jnp.pad(wkt, ((0, 0), (0, LANES - QK_NOPE), (0, 0))).astype(BF16)
    wkt_flat = wkt.reshape(N_HEADS * QK_NOPE, KV_LORA).astype(BF16)
    gpe_b = jnp.broadcast_to(g_k_head[0][QK_NOPE:].astype(F32)[:, None], (ROPE_DIM, LANES))
    wpool = w_pool[0].astype(BF16)
    pscale = row(pool_scale[0])
    wout_a = w_out[0][:ATTN_WIDTH].astype(BF16)
    wout_p = w_out[0][ATTN_WIDTH:].astype(BF16)
    wr_t = w_router[0].T.astype(F32)
    wr_hi = wr_t.astype(BF16)
    wr_lo = (wr_t - wr_hi.astype(F32)).astype(BF16)
    br_b = jnp.broadcast_to(b_router[0].astype(F32)[:, None], (N_EXPERTS, LANES))

    tabs_p = _rope_row_tables(jnp.arange(s, dtype=jnp.int32))
    pos_s = past + (jnp.arange(TM_PROJ, dtype=jnp.int32) % n_new)
    tabs_s = _rope_row_tables(pos_s)
    cos_t, sin_t = (t.T for t in _rope_angles(jnp.arange(past + LANES, dtype=jnp.int32)))
    tabs_attn = (cos_t[:, :past], sin_t[:, :past], cos_t[:, past:], sin_t[:, past:])

    ckv_p, kpe_p, q_p, k_p, v_p, xmid_p, pst_p = _proj_prompt(
        x_prompt, tabs_p,
        (row(g_mix[0]), w_main, row(g_q_a[0]), row(g_kv_a[0]), wq, gq128, wk, wv, gk128, wpool, pscale, wout_p))
    attn_p = _flash_prompt(q_p, k_p, v_p, b, s)

    xs_flat = x_sample.reshape(n_s, D_MODEL)
    ckv_s, kpe_s, u_s, qf_s, qlat_s = _proj_sample(
        xs_flat, tabs_s, (row(g_mix[0]), w_main, row(g_q_a[0]), row(g_kv_a[0]), wq, gq128, wkt128, gkn128))
    u_s3 = u_s.reshape(nb, n_new, POOL_WIDTH)
    pool_t = _pool_sample(jnp.transpose(state_pool[0], (1, 0, 2)), jnp.transpose(u_s3, (1, 0, 2)), wpool, pscale)
    pool_s = jnp.transpose(pool_t, (1, 0, 2)).reshape(n_s, POOL_WIDTH)
    kpe_new_t = _pad_lanes(jnp.transpose(kpe_s.reshape(nb, n_new, ROPE_DIM), (0, 2, 1)))
    attn_s = _sample_attn(page_table, cache_ckv[0], jnp.transpose(cache_kpe[0], (0, 2, 1)), qlat_s, qf_s,
                          ckv_s.reshape(nb, n_new, KV_LORA), kpe_new_t,
                          tabs_attn, (wkt_flat, gpe_b, wv)).reshape(n_s, ATTN_WIDTH)

    x1, h2, topi, gates, rank, counts = _post_attn(
        xmid_p, attn_p, xs_flat, attn_s, pool_s, (wout_a, wout_p, row(g_ffn[0]), wr_hi, wr_lo, br_b))
    nt = n_p + n_s
    counts = counts[:, 0]
    padded = (counts + TB - 1) // TB * TB
    ends = jnp.cumsum(padded)
    starts = ends - padded
    eids = jnp.arange(N_EXPERTS, dtype=jnp.int32)
    dest = (jnp.sum(jnp.where(topi[..., None] == eids, starts, 0), axis=-1) + rank).astype(jnp.int32)
    n_rows = nt * TOP_K + N_EXPERTS * TB
    n_blocks = n_rows // TB
    n_used = (ends[-1] // TB).astype(jnp.int32).reshape(1)
    blk_start = jnp.arange(n_blocks, dtype=jnp.int32) * TB
    blk_start = jnp.minimum(blk_start, jnp.maximum(ends[-1] - TB, 0))
    block_expert = jnp.minimum(jnp.sum((ends[None, :] <= blk_start[:, None]).astype(jnp.int32), axis=1),
                               N_EXPERTS - 1).astype(jnp.int32)

    row_token = _invert(dest, n_rows)
    b3 = lambda v: v[0].reshape(N_EXPERTS, 1, D_MODEL).astype(F32)
    ys = _experts(block_expert, n_used, row_token, h2, w_gate[0], b3(b_gate), w_up[0], b3(b_up), w_down[0], b3(b_down))
    gates_t = gates.T
    fin_w = (row(g_ple[0]), w_ple_gate[0].astype(BF16), w_ple_proj[0].astype(BF16))
    y_p = _final(dest, x1, gates_t, p_prompt[0].reshape(n_p, PLE_DIM), ys, fin_w, 0, n_p)
    y_s = _final(dest, x1, gates_t, p_sample[0].reshape(n_s, PLE_DIM), ys, fin_w, n_p // TM_TOK, n_s)

    pool_state_s = jnp.concatenate([state_pool[0][:, n_new:], u_s3], axis=1)
    return (y_p.reshape(b, s, D_MODEL), y_s.reshape(nb, n_new, D_MODEL),
            ckv_p.reshape(1, b, s, KV_LORA), kpe_p.reshape(1, b, s, ROPE_DIM), pst_p[:, HALO - POOL_PREV:][None],
            ckv_s.reshape(1, nb, n_new, KV_LORA), kpe_s.reshape(1, nb, n_new, ROPE_DIM), pool_state_s[None])
```
